```python
import jax, jax.numpy as jnp
from jax import lax
import numpy as np

D_MODEL = 1024
BATCH = 16
SEQ = 2048
DEPTH = 2
DEC_BATCH = 128
DEC_SEQ = 1
PAST_LEN = 8192
PAGE_SIZE = 128

HEAD_DIM = 64
ROT_DIM = HEAD_DIM // 4
ROPE_THETA = 500000.0
NSA_HEADS = 4
CMP_BLOCK = 32
SEL_BLOCK = 64
TOP_N = 8
WINDOW = 512
MLA_HEADS = 4
MLA_Q_LORA = 256
MLA_KV_LORA = 128
MLA_NOPE = 64
MLA_ROPE = 32
MLA_V = 64
FOX_HEADS = 4
BRANCH_W = 256
N_BRANCH = 3
N_GROUPS = 4
EXP_PER_GROUP = 8
N_EXPERTS = N_GROUPS * EXP_PER_GROUP
TOP_E = 2
EXPERT_FF = 256
QBLOCK = 128
LN_EPS = 1e-5
NEG_INF = -1e30
DN_ALPHA = (2 * DEPTH) ** 0.25
DN_BETA = (8 * DEPTH) ** -0.25
FOX_BIAS_INIT = 3.0
IN_WIDTHS = (NSA_HEADS * HEAD_DIM, 6 * HEAD_DIM, 3 * NSA_HEADS, MLA_Q_LORA, MLA_KV_LORA + MLA_ROPE,
             3 * FOX_HEADS * HEAD_DIM, FOX_HEADS, N_BRANCH * D_MODEL)
IN_DIM = sum(IN_WIDTHS)

kernel_name = 'hybrid_nsa_mla_fox_hmoe_step'


def _in_splits():
    return [int(v) for v in np.cumsum(IN_WIDTHS)[:-1]]


def _layernorm(x, g, b):
    xf = x.astype(jnp.float32)
    mu = jnp.mean(xf, -1, keepdims=True)
    var = jnp.mean(jnp.square(xf - mu), -1, keepdims=True)
    return ((xf - mu) * lax.rsqrt(var + LN_EPS)).astype(x.dtype) * g + b


def _rmsnorm(x, g):
    xf = x.astype(jnp.float32)
    return (xf * lax.rsqrt(jnp.mean(xf * xf, -1, keepdims=True) + LN_EPS)).astype(x.dtype) * g


def _rotary(x, pos, rot_dim):
    half = rot_dim // 2
    inv = 1.0 / (ROPE_THETA ** (jnp.arange(half, dtype=jnp.float32) * 2.0 / rot_dim))
    ang = pos.astype(jnp.float32)[:, None] * inv[None, :]
    cos = jnp.cos(ang)[None, :, None, :].astype(x.dtype)
    sin = jnp.sin(ang)[None, :, None, :].astype(x.dtype)
    x1, x2, rest = x[..., :half], x[..., half:rot_dim], x[..., rot_dim:]
    return jnp.concatenate([x1 * cos - x2 * sin, x1 * sin + x2 * cos, rest], axis=-1)


def _masked_softmax(s, mask):
    p = jax.nn.softmax(jnp.where(mask, s, NEG_INF), axis=-1)
    return jnp.where(mask, p, 0.0)


def _map_query_blocks(fn, qs, T):
    qb = QBLOCK if T % QBLOCK == 0 else T
    nb = T // qb
    split = lambda a: jnp.moveaxis(a.reshape(a.shape[0], nb, qb, *a.shape[2:]), 1, 0)
    starts = jnp.arange(nb, dtype=jnp.int32) * qb
    out = lax.map(lambda a: fn(a[1], *a[0]), (tuple(split(a) for a in qs), starts))
    return jnp.moveaxis(out, 0, 1).reshape(out.shape[1], T, *out.shape[3:])


def _nsa_attend(q, gates, kc, vc, ks, vs, kw, vw, cmp_pos, pos0):
    B, T, H, d = q.shape
    L = kc.shape[1]
    nbs = -(-L // SEL_BLOCK)
    nbc = nbs * (SEL_BLOCK // CMP_BLOCK)
    pad = nbs * SEL_BLOCK - L
    padl = lambda a: jnp.pad(a, ((0, 0), (0, pad), (0, 0)))
    a_k = jax.nn.softmax(cmp_pos[0].astype(jnp.float32)).astype(kc.dtype)
    a_v = jax.nn.softmax(cmp_pos[1].astype(jnp.float32)).astype(vc.dtype)
    ck = jnp.einsum('bnjd,j->bnd', padl(kc).reshape(B, nbc, CMP_BLOCK, d), a_k)
    cv = jnp.einsum('bnjd,j->bnd', padl(vc).reshape(B, nbc, CMP_BLOCK, d), a_v)
    cmp_end = (jnp.arange(nbc, dtype=jnp.int32) + 1) * CMP_BLOCK - 1
    ks_blk = padl(ks).reshape(B, nbs, SEL_BLOCK, d)
    vs_blk = padl(vs).reshape(B, nbs, SEL_BLOCK, d)
    n_sel = min(TOP_N, nbs)
    blk = jnp.arange(nbs, dtype=jnp.int32)
    bidx = jnp.arange(B)[:, None, None]
    scale = HEAD_DIM ** -0.5

    def block(start, qq, gg):
        qn = qq.shape[1]
        qpos = pos0 + start + jnp.arange(qn, dtype=jnp.int32)
        s_c = jnp.einsum('bthd,bnd->bhtn', qq, ck).astype(jnp.float32) * scale
        p_c = _masked_softmax(s_c, (cmp_end[None, :] <= qpos[:, None])[None, None])
        o_c = jnp.einsum('bhtn,bnd->bthd', p_c.astype(qq.dtype), cv)
        imp = p_c.sum(1).reshape(B, qn, nbs, SEL_BLOCK // CMP_BLOCK).sum(-1)
        cur = qpos // SEL_BLOCK
        valid = blk[None, :] * SEL_BLOCK <= qpos[:, None]
        forced = (blk[None, :] == 0) | (blk[None, :] == cur[:, None]) | (blk[None, :] == cur[:, None] - 1)
        score = jnp.where(forced & valid, 1e6, jnp.where(valid, imp, -1e6))
        _, idx = lax.top_k(score, n_sel)
        kg = ks_blk[bidx, idx]
        vg = vs_blk[bidx, idx]
        kpos = idx[..., None] * SEL_BLOCK + jnp.arange(SEL_BLOCK, dtype=jnp.int32)
        s_s = jnp.einsum('bthd,btnjd->bhtnj', qq, kg).astype(jnp.float32).reshape(B, H, qn, -1) * scale
        m_s = (kpos <= qpos[None, :, None, None]).reshape(B, 1, qn, -1)
        p_s = _masked_softmax(s_s, m_s).reshape(B, H, qn, n_sel, SEL_BLOCK)
        o_s = jnp.einsum('bhtnj,btnjd->bthd', p_s.astype(qq.dtype), vg)
        kwb = lax.dynamic_slice_in_dim(kw, start, WINDOW + qn, axis=1)
        vwb = lax.dynamic_slice_in_dim(vw, start, WINDOW + qn, axis=1)
        kposw = pos0 + start - WINDOW + jnp.arange(WINDOW + qn, dtype=jnp.int32)
        dist = qpos[:, None] - kposw[None, :]
        m_w = (dist >= 0) & (dist <= WINDOW) & (kposw[None, :] >= 0)
        s_w = jnp.einsum('bthd,bkd->bhtk', qq, kwb).astype(jnp.float32) * scale
        p_w = _masked_softmax(s_w, m_w[None, None])
        o_w = jnp.einsum('bhtk,bkd->bthd', p_w.astype(qq.dtype), vwb)
        return gg[:, :, 0, :, None] * o_c + gg[:, :, 1, :, None] * o_s + gg[:, :, 2, :, None] * o_w

    return _map_query_blocks(block, (q, gates), T)


def _mla_attend(q_lat, q_rope, c_kv, k_rope, pos0):
    T = q_lat.shape[1]
    kpos = jnp.arange(c_kv.shape[1], dtype=jnp.int32)
    scale = (MLA_NOPE + MLA_ROPE) ** -0.5

    def block(start, ql, qr):
        qpos = pos0 + start + jnp.arange(ql.shape[1], dtype=jnp.int32)
        s = (jnp.einsum('bthc,bsc->bhts', ql, c_kv) + jnp.einsum('bthr,bsr->bhts', qr, k_rope)).astype(jnp.float32) * scale
        p = _masked_softmax(s, (kpos[None, :] <= qpos[:, None])[None, None])
        return jnp.einsum('bhts,bsc->bthc', p.astype(c_kv.dtype), c_kv)

    return _map_query_blocks(block, (q_lat, q_rope), T)


def _fox_attend(q, k, v, logf_all, pos0):
    T = q.shape[1]
    F = jnp.cumsum(logf_all.astype(jnp.float32), axis=1)
    Ft = jnp.transpose(F, (0, 2, 1))
    Fq = F[:, pos0:pos0 + T]
    kpos = jnp.arange(k.shape[1], dtype=jnp.int32)
    scale = HEAD_DIM ** -0.5

    def block(start, qq, fq):
        qpos = pos0 + start + jnp.arange(qq.shape[1], dtype=jnp.int32)
        decay = jnp.transpose(fq, (0, 2, 1))[:, :, :, None] - Ft[:, :, None, :]
        s = jnp.einsum('bthd,bshd->bhts', qq, k).astype(jnp.float32) * scale + decay
        p = _masked_softmax(s, (kpos[None, :] <= qpos[:, None])[None, None])
        return jnp.einsum('bhts,bshd->bthd', p.astype(v.dtype), v)

    return _map_query_blocks(block, (q, Fq), T)


def _hier_moe(h, w_grp, b_grp, w_exp, b_exp, w_gate, w_up, w_down):
    B, T, D = h.shape
    hf = h.reshape(B * T, D)
    grp_logits = (hf @ w_grp + b_grp).astype(jnp.float32)
    g_idx = jnp.argmax(grp_logits, axis=-1)
    g_w = jnp.take_along_axis(jax.nn.softmax(grp_logits, axis=-1), g_idx[:, None], axis=-1)
    exp_logits = (hf @ w_exp + b_exp).astype(jnp.float32).reshape(-1, N_GROUPS, EXP_PER_GROUP)
    in_grp = jnp.take_along_axis(exp_logits, g_idx[:, None, None], axis=1)[:, 0]
    top_v, top_i = lax.top_k(in_grp, TOP_E)
    w_sel = jax.nn.softmax(top_v, axis=-1) * g_w
    e_idx = g_idx[:, None] * EXP_PER_GROUP + top_i
    dense_w = jnp.sum(jax.nn.one_hot(e_idx, N_EXPERTS, dtype=jnp.float32) * w_sel[..., None], axis=1).astype(h.dtype)
    out = jnp.zeros_like(hf)
    for e in range(N_EXPERTS):
        a = jax.nn.silu(hf @ w_gate[e]) * (hf @ w_up[e])
        out = out + dense_w[:, e:e + 1] * (a @ w_down[e])
    return out.reshape(B, T, D)


def _gather_pages(cache, l, page_table):
    g = cache[l, page_table]
    return g.reshape(g.shape[0], g.shape[1] * g.shape[2], *g.shape[3:])


def _trunk_layer(x, c, pos0, past, p):
    B, T, D = x.shape
    dt = x.dtype
    pos = pos0 + jnp.arange(T, dtype=jnp.int32)

    def with_past(name, rows):
        return rows if past is None else jnp.concatenate([past[name], rows], axis=1)

    ada = jnp.einsum('bd,sde->bse', jax.nn.silu(c), p['ada_w']) + p['ada_b']
    shift, scale, gate = jnp.split(ada, 3, axis=-1)
    h = x * (1.0 + scale[:, None, 0]) + shift[:, None, 0]
    nsa_q, nsa_kv, nsa_g, mla_dq, mla_dkv, fox_qkv, fox_f, br_g = jnp.split(h @ p['w_in'], _in_splits(), axis=-1)

    q = _rotary(nsa_q.reshape(B, T, NSA_HEADS, HEAD_DIM), pos, ROT_DIM)
    kc, vc, ks, vs, kw, vw = jnp.split(nsa_kv.reshape(B, T, 6, HEAD_DIM), 6, axis=2)
    kc, ks, kw = (_rotary(k, pos, ROT_DIM) for k in (kc, ks, kw))
    nsa_rows = jnp.concatenate([kc, vc, ks, vs], axis=2)
    win_rows = jnp.concatenate([kw, vw], axis=2)
    nsa_all = with_past('nsa_kv', nsa_rows)
    win_src = with_past('nsa_win', win_rows)
    n_keep = min(WINDOW, T) if past is None else past['nsa_win'].shape[1]
    win_pad = jnp.pad(win_src, ((0, 0), (WINDOW + T - win_src.shape[1], 0), (0, 0), (0, 0)))
    o_nsa = _nsa_attend(q, jax.nn.sigmoid(nsa_g.reshape(B, T, 3, NSA_HEADS)),
                        nsa_all[:, :, 0], nsa_all[:, :, 1], nsa_all[:, :, 2], nsa_all[:, :, 3],
                        win_pad[:, :, 0], win_pad[:, :, 1], p['nsa_cmp_pos'], pos0)
    o_nsa = o_nsa.reshape(B, T, NSA_HEADS * HEAD_DIM)

    qh = (_rmsnorm(mla_dq, p['mla_q_norm']) @ p['mla_w_uq']).reshape(B, T, MLA_HEADS, MLA_NOPE + MLA_ROPE)
    q_rope = _rotary(qh[..., MLA_NOPE:], pos, MLA_ROPE)
    q_lat = jnp.einsum('bthn,chn->bthc', qh[..., :MLA_NOPE], p['mla_w_uk'])
    lat_rows = _rmsnorm(mla_dkv[..., :MLA_KV_LORA], p['mla_kv_norm'])
    rope_rows = _rotary(mla_dkv[:, :, None, MLA_KV_LORA:], pos, MLA_ROPE)[:, :, 0]
    o_lat = _mla_attend(q_lat, q_rope, with_past('mla_lat', lat_rows), with_past('mla_rope', rope_rows), pos0)
    o_mla = jnp.einsum('bthc,chv->bthv', o_lat, p['mla_w_uv']).reshape(B, T, MLA_HEADS * MLA_V)

    fq, fk, fv = jnp.split(fox_qkv.reshape(B, T, 3, FOX_HEADS, HEAD_DIM), 3, axis=2)
    fox_rows = jnp.concatenate([fk, fv], axis=2)
    logf_rows = jax.nn.log_sigmoid((fox_f + p['fox_bf']).astype(jnp.float32)).astype(dt)
    fox_all = with_past('fox_kv', fox_rows)
    o_fox = _fox_attend(fq[:, :, 0], fox_all[:, :, 0], fox_all[:, :, 1], with_past('fox_logf', logf_rows), pos0)
    o_fox = o_fox.reshape(B, T, FOX_HEADS * HEAD_DIM)

    br = jnp.stack([o_nsa, o_mla, o_fox], axis=2)
    br = jnp.einsum('btnw,nwd->btnd', br, p['w_branch'])
    mix = jnp.sum(jax.nn.sigmoid(br_g.reshape(B, T, N_BRANCH, D)) * br, axis=2) @ p['w_o']
    x = _layernorm(DN_ALPHA * x + gate[:, None, 0] * mix, p['ln1_g'], p['ln1_b'])

    h2 = x * (1.0 + scale[:, None, 1]) + shift[:, None, 1]
    y_ff = _hier_moe(h2, p['moe_w_grp'], p['moe_b_grp'], p['moe_w_exp'], p['moe_b_exp'],
                     p['moe_w_gate'], p['moe_w_up'], p['moe_w_down'])
    x = _layernorm(DN_ALPHA * x + gate[:, None, 1] * y_ff, p['ln2_g'], p['ln2_b'])
    return x, (nsa_rows, win_src[:, -n_keep:], lat_rows, rope_rows, fox_rows, logf_rows)


def setup_inputs(seed: int = 0) -> dict:
    key = jax.random.key(seed)
    keys = iter(list(jax.random.split(key, 48)))
    nrm = lambda shape, s: jax.random.normal(next(keys), shape, jnp.float32) * s
    D = D_MODEL
    n_pages = PAST_LEN // PAGE_SIZE
    n_pool = (DEC_BATCH * n_pages * 5) // 4
    win_buf = min(WINDOW, PAST_LEN)
    inp = {}
    inp['x_prompt'] = nrm((BATCH, SEQ, D), 1.0)
    inp['x_sample'] = nrm((DEC_BATCH, DEC_SEQ, D), 1.0)
    inp['cache_nsa_kv'] = nrm((DEPTH, n_pool, PAGE_SIZE, 4, HEAD_DIM), 1.0)
    inp['state_nsa_win'] = nrm((DEPTH, DEC_BATCH, win_buf, 2, HEAD_DIM), 1.0)
    inp['cache_mla_latent'] = nrm((DEPTH, n_pool, PAGE_SIZE, MLA_KV_LORA), 1.0)
    inp['cache_mla_rope'] = nrm((DEPTH, n_pool, PAGE_SIZE, MLA_ROPE), 1.0)
    inp['cache_fox_kv'] = nrm((DEPTH, n_pool, PAGE_SIZE, 2, FOX_HEADS, HEAD_DIM), 1.0)
    inp['cache_fox_logf'] = jax.nn.log_sigmoid(FOX_BIAS_INIT + nrm((DEPTH, n_pool, PAGE_SIZE, FOX_HEADS), 1.0))
    perm = jax.random.permutation(next(keys), n_pool)[:DEC_BATCH * n_pages]
    inp['page_table'] = perm.reshape(DEC_BATCH, n_pages).astype(jnp.int32)
    inp['c_prompt'] = nrm((BATCH, D), 1.0)
    inp['c_sample'] = nrm((DEC_BATCH, D), 1.0)
    inp['ada_w'] = nrm((DEPTH, 2, D, 3 * D), 0.1 * D ** -0.5)
    inp['ada_b'] = jnp.concatenate([nrm((DEPTH, 2, 2 * D), 0.02), 1.0 + nrm((DEPTH, 2, D), 0.02)], axis=-1)
    inp['w_in'] = nrm((DEPTH, D, IN_DIM), D ** -0.5)
    inp['nsa_cmp_pos'] = nrm((DEPTH, 2, CMP_BLOCK), 0.1)
    inp['mla_q_norm'] = 1.0 + nrm((DEPTH, MLA_Q_LORA), 0.02)
    inp['mla_w_uq'] = nrm((DEPTH, MLA_Q_LORA, MLA_HEADS * (MLA_NOPE + MLA_ROPE)), MLA_Q_LORA ** -0.5)
    inp['mla_kv_norm'] = 1.0 + nrm((DEPTH, MLA_KV_LORA), 0.02)
    inp['mla_w_uk'] = nrm((DEPTH, MLA_KV_LORA, MLA_HEADS, MLA_NOPE), MLA_KV_LORA ** -0.5)
    inp['mla_w_uv'] = nrm((DEPTH, MLA_KV_LORA, MLA_HEADS, MLA_V), MLA_KV_LORA ** -0.5)
    inp['fox_bf'] = FOX_BIAS_INIT + nrm((DEPTH, FOX_HEADS), 0.1)
    inp['w_branch'] = nrm((DEPTH, N_BRANCH, BRANCH_W, D), BRANCH_W ** -0.5)
    inp['w_o'] = nrm((DEPTH, D, D), DN_BETA * D ** -0.5)
    inp['ln1_g'] = 1.0 + nrm((DEPTH, D), 0.02)
    inp['ln1_b'] = nrm((DEPTH, D), 0.02)
    inp['moe_w_grp'] = nrm((DEPTH, D, N_GROUPS), D ** -0.5)
    inp['moe_b_grp'] = nrm((DEPTH, N_GROUPS), 0.01)
    inp['moe_w_exp'] = nrm((DEPTH, D, N_EXPERTS), D ** -0.5)
    inp['moe_b_exp'] = nrm((DEPTH, N_EXPERTS), 0.01)
    inp['moe_w_gate'] = nrm((DEPTH, N_EXPERTS, D, EXPERT_FF), D ** -0.5)
    inp['moe_w_up'] = nrm((DEPTH, N_EXPERTS, D, EXPERT_FF), D ** -0.5)
    inp['moe_w_down'] = nrm((DEPTH, N_EXPERTS, EXPERT_FF, D), DN_BETA * EXPERT_FF ** -0.5)
    inp['ln2_g'] = 1.0 + nrm((DEPTH, D), 0.02)
    inp['ln2_b'] = nrm((DEPTH, D), 0.02)
    return inp


def reference(x_prompt, x_sample, cache_nsa_kv, state_nsa_win, cache_mla_latent, cache_mla_rope,
              cache_fox_kv, cache_fox_logf, page_table, c_prompt, c_sample, ada_w, ada_b, w_in,
              nsa_cmp_pos, mla_q_norm, mla_w_uq, mla_kv_norm, mla_w_uk, mla_w_uv, fox_bf, w_branch,
              w_o, ln1_g, ln1_b, moe_w_grp, moe_b_grp, moe_w_exp, moe_b_exp, moe_w_gate, moe_w_up,
              moe_w_down, ln2_g, ln2_b):
    past_len = page_table.shape[1] * PAGE_SIZE
    y_p, y_s = x_prompt, x_sample
    new_p = [[] for _ in range(6)]
    new_s = [[] for _ in range(6)]
    for l in range(DEPTH):
        p = {'ada_w': ada_w[l], 'ada_b': ada_b[l], 'w_in': w_in[l], 'nsa_cmp_pos': nsa_cmp_pos[l],
             'mla_q_norm': mla_q_norm[l], 'mla_w_uq': mla_w_uq[l], 'mla_kv_norm': mla_kv_norm[l],
             'mla_w_uk': mla_w_uk[l], 'mla_w_uv': mla_w_uv[l], 'fox_bf': fox_bf[l],
             'w_branch': w_branch[l], 'w_o': w_o[l], 'ln1_g': ln1_g[l], 'ln1_b': ln1_b[l],
             'moe_w_grp': moe_w_grp[l], 'moe_b_grp': moe_b_grp[l], 'moe_w_exp': moe_w_exp[l],
             'moe_b_exp': moe_b_exp[l], 'moe_w_gate': moe_w_gate[l], 'moe_w_up': moe_w_up[l],
             'moe_w_down': moe_w_down[l], 'ln2_g': ln2_g[l], 'ln2_b': ln2_b[l]}
        past = {'nsa_kv': _gather_pages(cache_nsa_kv, l, page_table),
                'nsa_win': state_nsa_win[l],
                'mla_lat': _gather_pages(cache_mla_latent, l, page_table),
                'mla_rope': _gather_pages(cache_mla_rope, l, page_table),
                'fox_kv': _gather_pages(cache_fox_kv, l, page_table),
                'fox_logf': _gather_pages(cache_fox_logf, l, page_table)}
        y_p, rows_p = _trunk_layer(y_p, c_prompt, 0, None, p)
        y_s, rows_s = _trunk_layer(y_s, c_sample, past_len, past, p)
        for i in range(6):
            new_p[i].append(rows_p[i])
            new_s[i].append(rows_s[i])
    p_nsa_kv, p_nsa_win, p_mla_latent, p_mla_rope, p_fox_kv, p_fox_logf = [jnp.stack(a) for a in new_p]
    s_nsa_kv, s_nsa_win, s_mla_latent, s_mla_rope, s_fox_kv, s_fox_logf = [jnp.stack(a) for a in new_s]
    return (y_p, y_s, p_nsa_kv, p_nsa_win, p_mla_latent, p_mla_rope, p_fox_kv, p_fox_logf,
            s_nsa_kv, s_nsa_win, s_mla_latent, s_mla_rope, s_fox_kv, s_fox_logf)
```

```python
import functools

import jax
import jax.numpy as jnp
from jax import lax
from jax.experimental import pallas as pl
from jax.experimental.pallas import tpu as pltpu

F32 = jnp.float32
BF16 = jnp.bfloat16
I32 = jnp.int32
HIGHEST = lax.Precision.HIGHEST

HEAD_DIM = 64
N_HEADS = 4
ROT_DIM = HEAD_DIM // 4
ROPE_THETA = 500000.0
CMP_BLOCK = 32
SEL_BLOCK = 64
TOP_N = 8
WINDOW = 512
MLA_Q_LORA = 256
MLA_KV_LORA = 128
MLA_NOPE = 64
MLA_ROPE = 32
MLA_V = 64
BRANCH_W = 256
N_BRANCH = 3
N_GROUPS = 4
EXP_PER_GROUP = 8
N_EXPERTS = N_GROUPS * EXP_PER_GROUP
LN_EPS = 1e-5
NEG_INF = -1e30
MLA_SCALE = (MLA_NOPE + MLA_ROPE) ** -0.5
ATT_SCALE = HEAD_DIM ** -0.5

LANES = 128
SUBLANES = 8
VMEM_LIMIT = 56 * 1024 * 1024
ROW_TILE = 256
MOE_TILE = 256

SEG_Q = 0
SEG_KV = 256
SEG_G = 640
SEG_DQ = 768
SEG_LAT = 1024
SEG_ROPE = 1152
SEG_FQKV = 1280
SEG_FF = 2048
SEG_BG = 2176
MLA_QW = 256


def _cparams(sem):
    return pltpu.CompilerParams(dimension_semantics=sem, vmem_limit_bytes=VMEM_LIMIT)


def _iota(shape, dim, dtype=I32):
    return lax.broadcasted_iota(dtype, shape, dim)


def _nt_dot(a, b):
    return lax.dot_general(a, b, (((1,), (1,)), ((), ())), preferred_element_type=F32)


def _dot(a, b, precision=None):
    return jnp.dot(a, b, preferred_element_type=F32, precision=precision)


def _ada_kernel(c_ref, w_ref, b_ref, o_ref):
    c = c_ref[...]
    s = c * jax.nn.sigmoid(c)
    o_ref[...] = _dot(s, w_ref[...], HIGHEST) + b_ref[...]


def _ada(c_all, ada_w, ada_b):
    S, D, D3 = ada_w.shape
    Bc = c_all.shape[0]
    tn = D3 // 3
    return pl.pallas_call(
        _ada_kernel,
        grid=(S, D3 // tn),
        in_specs=[pl.BlockSpec((Bc, D), lambda s, n: (0, 0)),
                  pl.BlockSpec((None, D, tn), lambda s, n: (s, 0, n)),
                  pl.BlockSpec((None, 1, tn), lambda s, n: (s, 0, n))],
        out_specs=pl.BlockSpec((None, Bc, tn), lambda s, n: (s, 0, n)),
        out_shape=jax.ShapeDtypeStruct((S, Bc, D3), F32),
        compiler_params=_cparams(("arbitrary", "arbitrary")),
        name="ada",
    )(c_all, ada_w, ada_b.reshape(S, 1, D3))


def _rot_tables(pos):
    posf = pos.astype(F32)[:, None]
    lane = jnp.arange(LANES)

    def table(period, half, rot_dim, active):
        i = lane % period
        in_rot = (i < rot_dim) & active
        inv = 1.0 / (ROPE_THETA ** ((i % half).astype(F32) * 2.0 / rot_dim))
        ang = posf * inv[None, :]
        cos = jnp.where(in_rot[None, :], jnp.cos(ang), 1.0)
        sin = jnp.where(in_rot[None, :], jnp.where((i < half)[None, :], -jnp.sin(ang), jnp.sin(ang)), 0.0)
        return [cos, sin]

    h = ROT_DIM // 2
    t = (table(HEAD_DIM, h, ROT_DIM, lane >= 0)
         + table(HEAD_DIM, h, ROT_DIM, (lane % LANES) < HEAD_DIM)
         + table(LANES, MLA_ROPE // 2, MLA_ROPE, lane >= 0)
         + table(MLA_ROPE, MLA_ROPE // 2, MLA_ROPE, lane >= 0))
    return jnp.stack(t).astype(F32)


def _inproj_kernel(x_ref, sh_ref, sc_ref, rot_ref, w_ref, qn_ref, wuq_ref, wqc_ref, kvn_ref, fbf_ref,
                   qn_o, rows_o, win_o, selwin_o, gate_o, mq_o, lat_o, rope_o, mkey_o,
                   fq_o, frows_o, fkv_o, logf_o, bg_o):
    D = x_ref.shape[-1]
    h = x_ref[...] * (1.0 + sc_ref[...]) + sh_ref[...]
    hb = h.astype(BF16)
    lane = _iota((1, LANES), 1)

    def proj(a, width):
        return _dot(hb, w_ref[:, a:a + width])

    def rot(xc, t, half, period):
        cos = rot_ref[2 * t]
        sin = rot_ref[2 * t + 1]
        fwd = pltpu.roll(xc, LANES - half, 1)
        bwd = pltpu.roll(xc, half, 1)
        first = (lane & (period - 1)) < half
        return xc * cos + jnp.where(first, fwd, bwd) * sin

    hr = ROT_DIM // 2
    q = proj(SEG_Q, 256)
    for c in range(2):
        qc = rot(q[:, c * LANES:(c + 1) * LANES], 0, hr, HEAD_DIM)
        qn_o[:, c * LANES:(c + 1) * LANES] = (qc * ATT_SCALE).astype(BF16)
    kv = proj(SEG_KV, 384)
    for c in range(3):
        r = rot(kv[:, c * LANES:(c + 1) * LANES], 1, hr, HEAD_DIM)
        if c < 2:
            rows_o[:, c * LANES:(c + 1) * LANES] = r
        else:
            win_o[...] = r
        if c >= 1:
            selwin_o[:, (c - 1) * LANES:c * LANES] = r.astype(BF16)
    gate_o[...] = jax.nn.sigmoid(proj(SEG_G, LANES))
    dq = proj(SEG_DQ, MLA_Q_LORA)
    dqn = dq * lax.rsqrt(jnp.mean(dq * dq, axis=-1, keepdims=True) + LN_EPS) * qn_ref[...]
    qh = _dot(dqn.astype(BF16), wuq_ref[...])
    qrope = rot(qh[:, 256:384], 3, MLA_ROPE // 2, MLA_ROPE)
    mq = _dot(qh[:, :256].astype(BF16), wqc_ref[0:256, :]) + _dot(qrope.astype(BF16), wqc_ref[256:384, :])
    mq_o[...] = (mq * MLA_SCALE).astype(BF16)
    lat = proj(SEG_LAT, 256)
    latc = lat[:, :LANES]
    latn = latc * lax.rsqrt(jnp.mean(latc * latc, axis=-1, keepdims=True) + LN_EPS) * kvn_ref[...]
    ropec = rot(lat[:, LANES:], 2, MLA_ROPE // 2, MLA_ROPE)
    lat_o[...] = latn
    rope_o[...] = ropec[:, :MLA_ROPE]
    mkey_o[:, :LANES] = latn.astype(BF16)
    mkey_o[:, LANES:] = ropec.astype(BF16)
    fqkv = proj(SEG_FQKV, 768)
    fq_o[...] = (fqkv[:, :256] * ATT_SCALE).astype(BF16)
    frows_o[...] = fqkv[:, 256:]
    fkv_o[...] = fqkv[:, 256:].astype(BF16)
    z = proj(SEG_FF, LANES) + fbf_ref[...]
    logf_o[...] = jnp.minimum(z, 0.0) - jnp.log(1.0 + jnp.exp(-jnp.abs(z)))
    for c in range(N_BRANCH):
        bg_o[:, c * D:(c + 1) * D] = jax.nn.sigmoid(proj(SEG_BG + c * D, D)).astype(BF16)


def _inproj(x, mod, rot, wts, tm):
    G, R, D = x.shape
    Rm = mod.shape[1]
    Rt = rot.shape[1]
    tmm = tm if Rm > 1 else 1
    tmt = tm if Rt > 1 else 1
    w_in, qn, wuq, wqc, kvn, fbf = wts
    widths = [(256, BF16), (256, F32), (128, F32), (256, BF16), (128, F32), (N_HEADS * MLA_QW, BF16),
              (128, F32), (MLA_ROPE, F32), (256, BF16), (256, BF16), (512, F32), (512, BF16), (128, F32),
              (N_BRANCH * D, BF16)]
    const = lambda g, i: (0, 0)
    row = lambda g, i: (g, i, 0)
    return pl.pallas_call(
        _inproj_kernel,
        grid=(G, R // tm),
        in_specs=[pl.BlockSpec((None, tm, D), row),
                  pl.BlockSpec((None, tmm, D), (lambda g, i: (g, i, 0)) if Rm > 1 else (lambda g, i: (g, 0, 0))),
                  pl.BlockSpec((None, tmm, D), (lambda g, i: (g, i, 1)) if Rm > 1 else (lambda g, i: (g, 0, 1))),
                  pl.BlockSpec((8, tmt, LANES), (lambda g, i: (0, i, 0)) if Rt > 1 else (lambda g, i: (0, 0, 0))),
                  pl.BlockSpec(w_in.shape, const),
                  pl.BlockSpec(qn.shape, const),
                  pl.BlockSpec(wuq.shape, const),
                  pl.BlockSpec(wqc.shape, const),
                  pl.BlockSpec(kvn.shape, const),
                  pl.BlockSpec(fbf.shape, const)],
        out_specs=[pl.BlockSpec((None, tm, w), row) for w, _ in widths],
        out_shape=[jax.ShapeDtypeStruct((G, R, w), dt) for w, dt in widths],
        compiler_params=_cparams(("arbitrary", "arbitrary")),
        name="inproj",
    )(x, mod, mod, rot, w_in, qn, wuq, wqc, kvn, fbf)


def _cumsum_kernel(lf_ref, fcol_ref, frow_ref, carry):
    tc = lf_ref.shape[0]

    @pl.when(pl.program_id(1) == 0)
    def _():
        carry[...] = jnp.zeros_like(carry)

    lower = (_iota((tc, tc), 1) <= _iota((tc, tc), 0)).astype(F32)
    f = _dot(lower, lf_ref[...], HIGHEST) + carry[...]
    fcol_ref[...] = f
    carry[...] = f[tc - 1:tc, :]
    frow_ref[...] = f.T[0:SUBLANES, :]


def _fox_cumsum(logf, tc):
    B, T, _ = logf.shape
    return pl.pallas_call(
        _cumsum_kernel,
        grid=(B, T // tc),
        in_specs=[pl.BlockSpec((None, tc, LANES), lambda b, i: (b, i, 0))],
        out_specs=[pl.BlockSpec((None, tc, LANES), lambda b, i: (b, i, 0)),
                   pl.BlockSpec((None, SUBLANES, tc), lambda b, i: (b, 0, i))],
        out_shape=[jax.ShapeDtypeStruct((B, T, LANES), F32),
                   jax.ShapeDtypeStruct((B, SUBLANES, T), F32)],
        scratch_shapes=[pltpu.VMEM((1, LANES), F32)],
        compiler_params=_cparams(("arbitrary", "arbitrary")),
        name="fox_cumsum",
    )(logf)


def _online_step(s, mask, v, m, l, acc):
    if mask is not None:
        s = jnp.where(mask, s, NEG_INF)
    m_new = jnp.maximum(m, jnp.max(s, axis=-1, keepdims=True))
    alpha = jnp.exp(m - m_new)
    p = jnp.exp(s - m_new)
    if mask is not None:
        p = jnp.where(mask, p, 0.0)
    l = alpha * l + jnp.sum(p, axis=-1, keepdims=True)
    acc = alpha * acc + _dot(p.astype(v.dtype), v)
    return m_new, l, acc


def _normalize(acc, l):
    return jnp.where(l > 0.0, acc / jnp.where(l > 0.0, l, 1.0), 0.0)


def _fox_kernel(q_ref, kv_ref, fcol_ref, frow_ref, o_ref, *, tk):
    tq = q_ref.shape[0]
    i = pl.program_id(1)
    qpos = i * tq + _iota((tq, 1), 0)
    fq_all = fcol_ref[...]
    outs = []
    for h in range(N_HEADS):
        lo = h * HEAD_DIM
        q = q_ref[:, lo:lo + HEAD_DIM]
        fq = fq_all[:, h:h + 1]

        def body(j, carry, lo=lo, q=q, fq=fq, h=h):
            ks = pl.multiple_of(j * tk, tk)
            k = kv_ref[pl.ds(ks, tk), lo:lo + HEAD_DIM]
            v = kv_ref[pl.ds(ks, tk), 256 + lo:256 + lo + HEAD_DIM]
            s = _nt_dot(q, k) + (fq - frow_ref[h:h + 1, pl.ds(ks, tk)])
            mask = (ks + _iota((1, tk), 1)) <= qpos
            return _online_step(s, mask, v, *carry)

        init = (jnp.full((tq, 1), NEG_INF, F32), jnp.zeros((tq, 1), F32), jnp.zeros((tq, HEAD_DIM), F32))
        m, l, acc = lax.fori_loop(0, (i * tq) // tk + tq // tk, body, init)
        outs.append(_normalize(acc, l))
    o_ref[...] = jnp.concatenate(outs, axis=-1).astype(o_ref.dtype)


def _fox_prompt(q, kv, fcol, frow, tq):
    B, T, _ = q.shape
    return pl.pallas_call(
        functools.partial(_fox_kernel, tk=tq),
        grid=(B, T // tq),
        in_specs=[pl.BlockSpec((None, tq, 256), lambda b, i: (b, i, 0)),
                  pl.BlockSpec((None, T, 512), lambda b, i: (b, 0, 0)),
                  pl.BlockSpec((None, tq, LANES), lambda b, i: (b, i, 0)),
                  pl.BlockSpec((None, SUBLANES, T), lambda b, i: (b, 0, 0))],
        out_specs=pl.BlockSpec((None, tq, 256), lambda b, i: (b, i, 0)),
        out_shape=jax.ShapeDtypeStruct((B, T, 256), BF16),
        compiler_params=_cparams(("arbitrary", "arbitrary")),
        name="fox_prompt",
    )(q, kv, fcol, frow)


def _mla_kernel(q_ref, key_ref, o_ref, *, tk):
    tq = q_ref.shape[0]
    i = pl.program_id(1)
    qpos = i * tq + _iota((tq, 1), 0)
    for h in range(N_HEADS):
        q = q_ref[:, h * MLA_QW:(h + 1) * MLA_QW]

        def body(j, carry, q=q):
            ks = pl.multiple_of(j * tk, tk)
            k = key_ref[pl.ds(ks, tk), :]
            mask = (ks + _iota((1, tk), 1)) <= qpos
            return _online_step(_nt_dot(q, k), mask, k[:, :MLA_KV_LORA], *carry)

        init = (jnp.full((tq, 1), NEG_INF, F32), jnp.zeros((tq, 1), F32), jnp.zeros((tq, MLA_KV_LORA), F32))
        m, l, acc = lax.fori_loop(0, (i * tq) // tk + tq // tk, body, init)
        o_ref[:, h * MLA_KV_LORA:(h + 1) * MLA_KV_LORA] = _normalize(acc, l).astype(o_ref.dtype)


def _mla_prompt(q, key, tq):
    B, T, _ = q.shape
    return pl.pallas_call(
        functools.partial(_mla_kernel, tk=tq),
        grid=(B, T // tq),
        in_specs=[pl.BlockSpec((None, tq, N_HEADS * MLA_QW), lambda b, i: (b, i, 0)),
                  pl.BlockSpec((None, T, MLA_QW), lambda b, i: (b, 0, 0))],
        out_specs=pl.BlockSpec((None, tq, N_HEADS * MLA_KV_LORA), lambda b, i: (b, i, 0)),
        out_shape=jax.ShapeDtypeStruct((B, T, N_HEADS * MLA_KV_LORA), BF16),
        compiler_params=_cparams(("arbitrary", "arbitrary")),
        name="mla_prompt",
    )(q, key)


def _select_blocks(score, blk_f, n_sel):
    nb = score.shape[-1]
    sel = jnp.zeros(score.shape, jnp.bool_)
    picks = []
    for _ in range(n_sel):
        mx = jnp.max(score, axis=-1, keepdims=True)
        idx = jnp.min(jnp.where(score == mx, blk_f, float(nb)), axis=-1, keepdims=True)
        pick = blk_f == idx
        sel = jnp.logical_or(sel, pick)
        score = jnp.where(pick, -jnp.inf, score)
        picks.append(idx)
    return sel, picks


def _block_scores(imp, blk, qpos):
    cur = qpos >> 6
    valid = blk * SEL_BLOCK <= qpos
    forced = (blk == 0) | (blk == cur) | (blk == cur - 1)
    return jnp.where(forced & valid, 1e6, jnp.where(valid, imp, -1e6))


def _nsa_kernel(q_ref, rows_ref, selwin_ref, gate_ref, cmpt_ref, o_ref, ckv_ref, *, tk):
    tq = q_ref.shape[0]
    T = rows_ref.shape[0]
    nbc = T // CMP_BLOCK
    nbs = T // SEL_BLOCK
    i = pl.program_id(1)

    @pl.when(i == 0)
    def _():
        ct = cmpt_ref[...]
        e = jnp.exp(ct - jnp.max(ct, axis=-1, keepdims=True))
        a = e / (jnp.sum(e, axis=-1, keepdims=True) / float(nbc))
        inblk = (_iota((nbc, T), 1) >> 5) == _iota((nbc, T), 0)
        rows = rows_ref[...]
        ck = _dot(jnp.where(inblk, a[0:1, :], 0.0), rows, HIGHEST)
        cv = _dot(jnp.where(inblk, a[1:2, :], 0.0), rows, HIGHEST)
        ckv_ref[...] = jnp.where(_iota((1, LANES), 1) < HEAD_DIM, ck, cv)

    qpos = i * tq + _iota((tq, 1), 0)
    ckv = ckv_ref[...].astype(BF16)
    ck = ckv[:, :HEAD_DIM]
    cv = ckv[:, HEAD_DIM:]
    cblk = _iota((1, nbc), 1)
    cmask = (cblk + 1) * CMP_BLOCK - 1 <= qpos
    o_c = []
    psum = jnp.zeros((tq, nbc), F32)
    qs = [q_ref[:, h * HEAD_DIM:(h + 1) * HEAD_DIM] for h in range(N_HEADS)]
    for h in range(N_HEADS):
        s = jnp.where(cmask, _nt_dot(qs[h], ck), NEG_INF)
        p = jnp.where(cmask, jnp.exp(s - jnp.max(s, axis=-1, keepdims=True)), 0.0)
        den = jnp.sum(p, axis=-1, keepdims=True)
        p = p * jnp.where(den > 0.0, 1.0 / jnp.where(den > 0.0, den, 1.0), 0.0)
        o_c.append(_dot(p.astype(BF16), cv))
        psum = psum + p
    pair = ((_iota((nbc, nbs), 0) >> 1) == _iota((nbc, nbs), 1)).astype(F32)
    imp = _dot(psum, pair, HIGHEST)
    blk = _iota((1, nbs), 1)
    sel, _ = _select_blocks(_block_scores(imp, blk, qpos), blk.astype(F32), min(TOP_N, nbs))
    sel_b = sel.astype(BF16)

    def init():
        return tuple((jnp.full((tq, 1), NEG_INF, F32), jnp.zeros((tq, 1), F32),
                      jnp.zeros((tq, HEAD_DIM), F32)) for _ in range(N_HEADS))

    def sel_body(j, carry):
        ks = pl.multiple_of(j * tk, tk)
        kpos = ks + _iota((1, tk), 1)
        expand = (((ks + _iota((nbs, tk), 1)) >> 6) == _iota((nbs, tk), 0)).astype(BF16)
        mask = (_dot(sel_b, expand) > 0.5) & (kpos <= qpos)
        k = selwin_ref[pl.ds(ks, tk), 0:HEAD_DIM]
        v = selwin_ref[pl.ds(ks, tk), HEAD_DIM:2 * HEAD_DIM]
        return tuple(_online_step(_nt_dot(qs[h], k), mask, v, *carry[h]) for h in range(N_HEADS))

    n_kv = (i * tq) // tk + tq // tk
    st_s = lax.fori_loop(0, n_kv, sel_body, init())

    def win_body(j, carry):
        ks = pl.multiple_of(j * tk, tk)
        dist = qpos - (ks + _iota((1, tk), 1))
        mask = (dist >= 0) & (dist <= WINDOW)
        k = selwin_ref[pl.ds(ks, tk), 2 * HEAD_DIM:3 * HEAD_DIM]
        v = selwin_ref[pl.ds(ks, tk), 3 * HEAD_DIM:4 * HEAD_DIM]
        return tuple(_online_step(_nt_dot(qs[h], k), mask, v, *carry[h]) for h in range(N_HEADS))

    first = jnp.maximum((i * tq - WINDOW) // tk, 0)
    st_w = lax.fori_loop(first, n_kv, win_body, init())

    g = gate_ref[...]
    outs = []
    for h in range(N_HEADS):
        o_s = _normalize(st_s[h][2], st_s[h][1])
        o_w = _normalize(st_w[h][2], st_w[h][1])
        outs.append(g[:, h:h + 1] * o_c[h] + g[:, N_HEADS + h:N_HEADS + h + 1] * o_s
                    + g[:, 2 * N_HEADS + h:2 * N_HEADS + h + 1] * o_w)
    o_ref[...] = jnp.concatenate(outs, axis=-1).astype(o_ref.dtype)


def _nsa_prompt(q, rows, selwin, gates, cmp_t, tq):
    B, T, _ = q.shape
    assert T % SEL_BLOCK == 0 and T % tq == 0
    return pl.pallas_call(
        functools.partial(_nsa_kernel, tk=tq),
        grid=(B, T // tq),
        in_specs=[pl.BlockSpec((None, tq, 256), lambda b, i: (b, i, 0)),
                  pl.BlockSpec((None, T, LANES), lambda b, i: (b, 0, 0)),
                  pl.BlockSpec((None, T, 256), lambda b, i: (b, 0, 0)),
                  pl.BlockSpec((None, tq, LANES), lambda b, i: (b, i, 0)),
                  pl.BlockSpec((2, T), lambda b, i: (0, 0))],
        out_specs=pl.BlockSpec((None, tq, 256), lambda b, i: (b, i, 0)),
        out_shape=jax.ShapeDtypeStruct((B, T, 256), BF16),
        scratch_shapes=[pltpu.VMEM((T // CMP_BLOCK, LANES), F32)],
        compiler_params=_cparams(("arbitrary", "arbitrary")),
        name="nsa_prompt",
    )(q, rows, selwin, gates, cmp_t)


def _pages_per_step(n_pages):
    for g in (8, 4):
        if n_pages % g == 0:
            return g
    raise ValueError("number of pages per sequence must be a multiple of 4")


def _fox_dec_kernel(pt_ref, q_ref, new_ref, lfn_ref, *refs, G):
    kv_refs = refs[:G]
    lf_refs = refs[G:2 * G]
    o_ref = refs[2 * G]
    m_ref, l_ref, acc_ref, car_ref = refs[2 * G + 1:]
    s_id = pl.program_id(1)
    q = q_ref[...]

    @pl.when(s_id == 0)
    def _():
        new = new_ref[...]
        m_ref[...] = jnp.sum(q * new[:, :256], axis=-1, keepdims=True)
        l_ref[...] = jnp.ones_like(l_ref)
        acc_ref[...] = jnp.broadcast_to(new[:, 256:], acc_ref.shape)
        car_ref[...] = lfn_ref[...]

    qb = q.astype(BF16)
    later = (_iota((LANES, LANES), 0) > _iota((LANES, LANES), 1)).astype(F32)
    for g in range(G):
        kv = kv_refs[g][...]
        lf = lf_refs[g][...]
        bias = _dot(lf, later, HIGHEST) + car_ref[...]
        s = _nt_dot(qb, kv[:, :256].astype(BF16)) + bias
        m, l, acc = _online_step(s, None, kv[:, 256:].astype(BF16), m_ref[...], l_ref[...], acc_ref[...])
        m_ref[...] = m
        l_ref[...] = l
        acc_ref[...] = acc
        car_ref[...] = car_ref[...] + jnp.sum(lf, axis=-1, keepdims=True)

    @pl.when(s_id == pl.num_programs(1) - 1)
    def _():
        o = acc_ref[...] / l_ref[...]
        own = (_iota(o.shape, 1) >> 6) == _iota(o.shape, 0)
        o_ref[...] = jnp.sum(jnp.where(own, o, 0.0), axis=0, keepdims=True)


def _fox_decode(layer, page_table, q_bd, new_row, lf_new, cache_kv, cache_lf_t):
    DB, NP = page_table.shape
    G = _pages_per_step(NP)
    pt = page_table.reshape(-1)

    def page(g):
        return lambda b, s, pt: (layer, pt[b * NP + NP - 1 - (s * G + g)], 0, 0)

    grid_spec = pltpu.PrefetchScalarGridSpec(
        num_scalar_prefetch=1,
        grid=(DB, NP // G),
        in_specs=([pl.BlockSpec((None, SUBLANES, 256), lambda b, s, pt: (b, 0, 0)),
                   pl.BlockSpec((None, 1, 512), lambda b, s, pt: (b, 0, 0)),
                   pl.BlockSpec((None, SUBLANES, 1), lambda b, s, pt: (b, 0, 0))]
                  + [pl.BlockSpec((None, None, LANES, 512), page(g)) for g in range(G)]
                  + [pl.BlockSpec((None, None, SUBLANES, LANES), page(g)) for g in range(G)]),
        out_specs=pl.BlockSpec((None, 1, 256), lambda b, s, pt: (b, 0, 0)),
        scratch_shapes=[pltpu.VMEM((SUBLANES, 1), F32), pltpu.VMEM((SUBLANES, 1), F32),
                        pltpu.VMEM((SUBLANES, 256), F32), pltpu.VMEM((SUBLANES, 1), F32)])
    return pl.pallas_call(
        functools.partial(_fox_dec_kernel, G=G),
        grid_spec=grid_spec,
        out_shape=jax.ShapeDtypeStruct((DB, 1, 256), F32),
        compiler_params=_cparams(("arbitrary", "arbitrary")),
        name="fox_decode",
    )(pt, q_bd, new_row, lf_new, *([cache_kv] * G), *([cache_lf_t] * G))


def _mla_dec_kernel(pt_ref, q_ref, new_ref, *refs, G):
    lat_refs = refs[:G]
    rope_refs = refs[G:2 * G]
    o_ref = refs[2 * G]
    m_ref, l_ref, acc_ref, key_ref = refs[2 * G + 1:]
    s_id = pl.program_id(1)
    q = q_ref[...]

    @pl.when(s_id == 0)
    def _():
        new = new_ref[...]
        m_ref[...] = jnp.sum(q * new, axis=-1, keepdims=True)
        l_ref[...] = jnp.ones_like(l_ref)
        acc_ref[...] = jnp.broadcast_to(new[:, :MLA_KV_LORA], acc_ref.shape)

    qb = q.astype(BF16)
    for g in range(G):
        key_ref[...] = jnp.zeros_like(key_ref)
        key_ref[:, :MLA_KV_LORA] = lat_refs[g][...].astype(BF16)
        key_ref[:, MLA_KV_LORA:MLA_KV_LORA + MLA_ROPE] = rope_refs[g][...].astype(BF16)
        k = key_ref[...]
        m, l, acc = _online_step(_nt_dot(qb, k), None, k[:, :MLA_KV_LORA], m_ref[...], l_ref[...], acc_ref[...])
        m_ref[...] = m
        l_ref[...] = l
        acc_ref[...] = acc

    @pl.when(s_id == pl.num_programs(1) - 1)
    def _():
        o_ref[...] = acc_ref[...] / l_ref[...]


def _mla_decode(layer, page_table, q_h, new_key, cache_lat, cache_rope):
    DB, NP = page_table.shape
    G = _pages_per_step(NP)
    pt = page_table.reshape(-1)

    def page(g):
        return lambda b, s, pt: (layer, pt[b * NP + s * G + g], 0, 0)

    grid_spec = pltpu.PrefetchScalarGridSpec(
        num_scalar_prefetch=1,
        grid=(DB, NP // G),
        in_specs=([pl.BlockSpec((None, SUBLANES, MLA_QW), lambda b, s, pt: (b, 0, 0)),
                   pl.BlockSpec((None, 1, MLA_QW), lambda b, s, pt: (b, 0, 0))]
                  + [pl.BlockSpec((None, None, LANES, MLA_KV_LORA), page(g)) for g in range(G)]
                  + [pl.BlockSpec((None, None, LANES, MLA_ROPE), page(g)) for g in range(G)]),
        out_specs=pl.BlockSpec((None, SUBLANES, MLA_KV_LORA), lambda b, s, pt: (b, 0, 0)),
        scratch_shapes=[pltpu.VMEM((SUBLANES, 1), F32), pltpu.VMEM((SUBLANES, 1), F32),
                        pltpu.VMEM((SUBLANES, MLA_KV_LORA), F32), pltpu.VMEM((LANES, MLA_QW), BF16)])
    return pl.pallas_call(
        functools.partial(_mla_dec_kernel, G=G),
        grid_spec=grid_spec,
        out_shape=jax.ShapeDtypeStruct((DB, SUBLANES, MLA_KV_LORA), F32),
        compiler_params=_cparams(("arbitrary", "arbitrary")),
        name="mla_decode",
    )(pt, q_h, new_key, *([cache_lat] * G), *([cache_rope] * G))


def _nsa_cmp_kernel(pt_ref, q_ref, cmpt_ref, *refs, G, past_len):
    page_refs = refs[:G]
    sel_ref, oc_ref = refs[G:G + 2]
    ckv_ref = refs[G + 2]
    s_id = pl.program_id(1)
    per_page = LANES // CMP_BLOCK
    ct = cmpt_ref[...]
    e = jnp.exp(ct - jnp.max(ct, axis=-1, keepdims=True))
    a = e / (jnp.sum(e, axis=-1, keepdims=True) / float(per_page))
    inblk = (_iota((per_page, LANES), 1) >> 5) == _iota((per_page, LANES), 0)
    wk = jnp.where(inblk, a[0:1, :], 0.0)
    wv = jnp.where(inblk, a[1:2, :], 0.0)
    is_k = _iota((1, LANES), 1) < HEAD_DIM
    parts = []
    for g in range(G):
        page = page_refs[g][...]
        parts.append(jnp.where(is_k, _dot(wk, page, HIGHEST), _dot(wv, page, HIGHEST)))
    rows = G * per_page
    ckv_ref[pl.ds(pl.multiple_of(s_id * rows, rows), rows), :] = jnp.concatenate(parts, axis=0)

    @pl.when(s_id == pl.num_programs(1) - 1)
    def _():
        nbc = ckv_ref.shape[0]
        nb_past = nbc // (SEL_BLOCK // CMP_BLOCK)
        width = sel_ref.shape[-1]
        qpos = past_len
        ckv = ckv_ref[...].astype(BF16)
        q = q_ref[...].astype(BF16)
        cmask = (_iota((1, nbc), 1) + 1) * CMP_BLOCK - 1 <= qpos
        s = jnp.where(cmask, _nt_dot(q, ckv[:, :HEAD_DIM]), NEG_INF)
        p = jnp.where(cmask, jnp.exp(s - jnp.max(s, axis=-1, keepdims=True)), 0.0)
        den = jnp.sum(p, axis=-1, keepdims=True)
        p = p * jnp.where(den > 0.0, 1.0 / jnp.where(den > 0.0, den, 1.0), 0.0)
        oc_ref[...] = _dot(p.astype(BF16), ckv[:, HEAD_DIM:])
        head = _iota((SUBLANES, 1), 0) < N_HEADS
        psum = jnp.sum(jnp.where(head, p, 0.0), axis=0, keepdims=True)
        pair = ((_iota((nbc, width), 0) >> 1) == _iota((nbc, width), 1)).astype(F32)
        imp = _dot(jnp.broadcast_to(psum, (SUBLANES, nbc)), pair, HIGHEST)[0:1, :]
        blk = _iota((1, width), 1)
        n_sel = min(TOP_N, nb_past + 1)
        _, picks = _select_blocks(_block_scores(imp, blk, qpos), blk.astype(F32), n_sel)
        out = jnp.full((1, width), -1.0, F32)
        for r, idx in enumerate(picks):
            out = jnp.where(blk == r, idx, out)
        sel_ref[...] = out.astype(I32)


def _nsa_cmp_decode(layer, page_table, q_h, cmp_page, cache_cmp, past_len):
    DB, NP = page_table.shape
    G = _pages_per_step(NP)
    pt = page_table.reshape(-1)
    nbc = NP * (LANES // CMP_BLOCK)
    nb_past = nbc // 2
    width = (nb_past + 1 + LANES - 1) // LANES * LANES

    def page(g):
        return lambda b, s, pt: (layer, pt[b * NP + s * G + g], 0, 0)

    grid_spec = pltpu.PrefetchScalarGridSpec(
        num_scalar_prefetch=1,
        grid=(DB, NP // G),
        in_specs=([pl.BlockSpec((None, SUBLANES, HEAD_DIM), lambda b, s, pt: (b, 0, 0)),
                   pl.BlockSpec((2, LANES), lambda b, s, pt: (0, 0))]
                  + [pl.BlockSpec((None, None, LANES, LANES), page(g)) for g in range(G)]),
        out_specs=[pl.BlockSpec((None, 1, width), lambda b, s, pt: (b, 0, 0)),
                   pl.BlockSpec((None, SUBLANES, HEAD_DIM), lambda b, s, pt: (b, 0, 0))],
        scratch_shapes=[pltpu.VMEM((nbc, LANES), F32)])
    return pl.pallas_call(
        functools.partial(_nsa_cmp_kernel, G=G, past_len=past_len),
        grid_spec=grid_spec,
        out_shape=[jax.ShapeDtypeStruct((DB, 1, width), I32),
                   jax.ShapeDtypeStruct((DB, SUBLANES, HEAD_DIM), F32)],
        compiler_params=_cparams(("arbitrary", "arbitrary")),
        name="nsa_cmp_decode",
    )(pt, q_h, cmp_page, *([cache_cmp] * G))


def _nsa_sel_kernel(pt_ref, sel_ref, q_ref, oc_ref, new_ref, gate_ref, blk_ref, win_ref, o_ref, buf_ref,
                    *, n_sel, nb_past):
    b = pl.program_id(0)
    k_id = pl.program_id(1)
    buf_ref[pl.ds(pl.multiple_of(k_id * SEL_BLOCK, SEL_BLOCK), SEL_BLOCK), :] = blk_ref[...]

    @pl.when(k_id == n_sel - 1)
    def _():
        q = q_ref[...]
        qb = q.astype(BF16)
        new = new_ref[...]
        nkeys = n_sel * SEL_BLOCK
        slot = _iota((1, nkeys), 1) >> 6
        blk_of_key = jnp.full((1, nkeys), -1, I32)
        n_new = jnp.int32(0)
        for r in range(n_sel):
            blk = sel_ref[b * n_sel + r]
            blk_of_key = jnp.where(slot == r, blk, blk_of_key)
            n_new = n_new + jnp.where(blk == nb_past, 1, 0)
        past_ok = (blk_of_key >= 0) & (blk_of_key < nb_past)
        has_new = jnp.full((1, 1), n_new, I32) > 0

        def attend(keys, vals, mask, k_new, v_new, new_ok):
            s = _nt_dot(qb, keys.astype(BF16))
            if mask is not None:
                s = jnp.where(mask, s, NEG_INF)
            s_new = jnp.where(new_ok, jnp.sum(q * k_new, axis=-1, keepdims=True), NEG_INF)
            m = jnp.maximum(jnp.max(s, axis=-1, keepdims=True), s_new)
            p = jnp.exp(s - m)
            if mask is not None:
                p = jnp.where(mask, p, 0.0)
            p_new = jnp.where(new_ok, jnp.exp(s_new - m), 0.0)
            l = jnp.sum(p, axis=-1, keepdims=True) + p_new
            return _normalize(_dot(p.astype(BF16), vals.astype(BF16)) + p_new * v_new, l)

        buf = buf_ref[...]
        o_s = attend(buf[:, :HEAD_DIM], buf[:, HEAD_DIM:], past_ok,
                     new[:, 0:HEAD_DIM], new[:, HEAD_DIM:2 * HEAD_DIM], has_new)
        win = win_ref[...]
        o_w = attend(win[:, :HEAD_DIM], win[:, HEAD_DIM:], None,
                     new[:, 2 * HEAD_DIM:3 * HEAD_DIM], new[:, 3 * HEAD_DIM:], jnp.ones((1, 1), jnp.bool_))
        g = gate_ref[...]
        o_ref[...] = g[:, 0:1] * oc_ref[...] + g[:, 1:2] * o_s + g[:, 2:3] * o_w


def _nsa_sel_decode(layer, page_table, sel, q_h, o_c, new_row, gates_t, cache_sel, state_win):
    DB, NP = page_table.shape
    nb_past = NP * (LANES // SEL_BLOCK)
    n_sel = min(TOP_N, nb_past + 1)
    WB = state_win.shape[2]
    pt = page_table.reshape(-1)
    sel_flat = sel[:, 0, :n_sel].reshape(-1)

    def block(b, k, pt, sel):
        blk = jnp.clip(sel[b * n_sel + k], 0, nb_past - 1)
        return (layer, pt[b * NP + blk // 2], blk % 2, 1)

    row = lambda b, k, pt, sel: (b, 0, 0)
    grid_spec = pltpu.PrefetchScalarGridSpec(
        num_scalar_prefetch=2,
        grid=(DB, n_sel),
        in_specs=[pl.BlockSpec((None, SUBLANES, HEAD_DIM), row),
                  pl.BlockSpec((None, SUBLANES, HEAD_DIM), row),
                  pl.BlockSpec((None, 1, 256), row),
                  pl.BlockSpec((None, SUBLANES, LANES), row),
                  pl.BlockSpec((None, None, SEL_BLOCK, LANES), block),
                  pl.BlockSpec((None, None, WB, LANES), lambda b, k, pt, sel: (layer, b, 0, 0))],
        out_specs=pl.BlockSpec((None, SUBLANES, HEAD_DIM), row),
        scratch_shapes=[pltpu.VMEM((n_sel * SEL_BLOCK, LANES), F32)])
    return pl.pallas_call(
        functools.partial(_nsa_sel_kernel, n_sel=n_sel, nb_past=nb_past),
        grid_spec=grid_spec,
        out_shape=jax.ShapeDtypeStruct((DB, SUBLANES, HEAD_DIM), F32),
        compiler_params=_cparams(("arbitrary", "arbitrary")),
        name="nsa_sel_decode",
    )(pt, sel_flat, q_h, o_c, new_row, gates_t, cache_sel, state_win)


def _layernorm(y, g, b):
    mu = jnp.mean(y, axis=-1, keepdims=True)
    d = y - mu
    var = jnp.mean(d * d, axis=-1, keepdims=True)
    return d * lax.rsqrt(var + LN_EPS) * g + b


def _merge_kernel(x_ref, onsa_ref, olat_ref, ofox_ref, bg_ref, g1_ref, sh2_ref, sc2_ref,
                  wuv_ref, wb_ref, wo_ref, lng_ref, lnb_ref, wr_ref, br_ref,
                  x1_o, h2_o, ri_o, rw_o, *, alpha):
    D = x_ref.shape[-1]
    o_mla = _dot(olat_ref[...], wuv_ref[...]).astype(BF16)
    branches = (onsa_ref[...], o_mla, ofox_ref[...])
    mix = None
    for n in range(N_BRANCH):
        t = bg_ref[:, n * D:(n + 1) * D].astype(F32) * _dot(branches[n], wb_ref[n])
        mix = t if mix is None else mix + t
    mix = _dot(mix.astype(BF16), wo_ref[...])
    x1 = _layernorm(alpha * x_ref[...] + g1_ref[...] * mix, lng_ref[...], lnb_ref[...])
    x1_o[...] = x1
    h2 = x1 * (1.0 + sc2_ref[...]) + sh2_ref[...]
    h2_o[...] = h2.astype(BF16)
    logits = _dot(h2, wr_ref[...], HIGHEST) + br_ref[...]
    lane = _iota((1, LANES), 1)
    lane_f = lane.astype(F32)
    is_g = lane < N_GROUPS
    gmax = jnp.max(jnp.where(is_g, logits, NEG_INF), axis=-1, keepdims=True)
    gidx = jnp.min(jnp.where(is_g & (logits == gmax), lane_f, float(LANES)), axis=-1, keepdims=True)
    g_w = 1.0 / jnp.sum(jnp.where(is_g, jnp.exp(logits - gmax), 0.0), axis=-1, keepdims=True)
    grp_of_lane = ((lane - N_GROUPS) >> 3).astype(F32)
    in_grp = (lane >= N_GROUPS) & (lane < N_GROUPS + N_EXPERTS) & (grp_of_lane == gidx)
    v1 = jnp.max(jnp.where(in_grp, logits, NEG_INF), axis=-1, keepdims=True)
    i1 = jnp.min(jnp.where(in_grp & (logits == v1), lane_f, float(LANES)), axis=-1, keepdims=True)
    rest = in_grp & (lane_f != i1)
    v2 = jnp.max(jnp.where(rest, logits, NEG_INF), axis=-1, keepdims=True)
    i2 = jnp.min(jnp.where(rest & (logits == v2), lane_f, float(LANES)), axis=-1, keepdims=True)
    e21 = jnp.exp(v2 - v1)
    w1 = g_w / (1.0 + e21)
    w2 = g_w * e21 / (1.0 + e21)
    ri_o[...] = jnp.where(lane == 0, i1 - N_GROUPS, jnp.where(lane == 1, i2 - N_GROUPS, 0.0)).astype(I32)
    rw_o[...] = jnp.where(lane == 0, w1, jnp.where(lane == 1, w2, 0.0))


def _merge(x, o_nsa, o_lat, o_fox, bg, mod0, mod1, wts, alpha, tm):
    G, R, D = x.shape
    Rm = mod0.shape[1]
    tmm = tm if Rm > 1 else 1
    wuv, wb, wo, lng, lnb, wr, br = wts
    row = lambda g, i: (g, i, 0)

    def modspec(j):
        return pl.BlockSpec((None, tmm, D), (lambda g, i: (g, i, j)) if Rm > 1 else (lambda g, i: (g, 0, j)))

    def const(a):
        return pl.BlockSpec(a.shape, lambda g, i: (0,) * a.ndim)

    return pl.pallas_call(
        functools.partial(_merge_kernel, alpha=alpha),
        grid=(G, R // tm),
        in_specs=[pl.BlockSpec((None, tm, D), row),
                  pl.BlockSpec((None, tm, 256), row),
                  pl.BlockSpec((None, tm, N_HEADS * MLA_KV_LORA), row),
                  pl.BlockSpec((None, tm, 256), row),
                  pl.BlockSpec((None, tm, N_BRANCH * D), row),
                  modspec(2), ] + [pl.BlockSpec((None, tmm, D), (lambda g, i: (g, i, 0)) if Rm > 1 else (lambda g, i: (g, 0, 0))),
                                   pl.BlockSpec((None, tmm, D), (lambda g, i: (g, i, 1)) if Rm > 1 else (lambda g, i: (g, 0, 1)))]
                 + [const(a) for a in (wuv, wb, wo, lng, lnb, wr, br)],
        out_specs=[pl.BlockSpec((None, tm, D), row), pl.BlockSpec((None, tm, D), row),
                   pl.BlockSpec((None, tm, LANES), row), pl.BlockSpec((None, tm, LANES), row)],
        out_shape=[jax.ShapeDtypeStruct((G, R, D), F32), jax.ShapeDtypeStruct((G, R, D), BF16),
                   jax.ShapeDtypeStruct((G, R, LANES), I32), jax.ShapeDtypeStruct((G, R, LANES), F32)],
        compiler_params=_cparams(("arbitrary", "arbitrary")),
        name="merge",
    )(x, o_nsa, o_lat, o_fox, bg, mod0, mod1, mod1, wuv, wb, wo, lng, lnb, wr, br)


def _ffn_kernel(te_ref, nt_ref, x_ref, wg_ref, wu_ref, wd_ref, o_ref):
    i = pl.program_id(0)

    @pl.when(i < nt_ref[0])
    def _():
        x = x_ref[...]
        gte = _dot(x, wg_ref[...])
        a = gte * jax.nn.sigmoid(gte) * _dot(x, wu_ref[...])
        o_ref[...] = _dot(a.astype(BF16), wd_ref[...]).astype(o_ref.dtype)

    @pl.when(i >= nt_ref[0])
    def _():
        o_ref[...] = jnp.zeros_like(o_ref)


def _expert_ffn(x_sorted, tile_expert, n_tiles_used, wg, wu, wd):
    NP, D = x_sorted.shape
    E, _, FF = wg.shape
    grid_spec = pltpu.PrefetchScalarGridSpec(
        num_scalar_prefetch=2,
        grid=(NP // MOE_TILE,),
        in_specs=[pl.BlockSpec((MOE_TILE, D), lambda i, te, nt: (i, 0)),
                  pl.BlockSpec((None, D, FF), lambda i, te, nt: (te[i], 0, 0)),
                  pl.BlockSpec((None, D, FF), lambda i, te, nt: (te[i], 0, 0)),
                  pl.BlockSpec((None, FF, D), lambda i, te, nt: (te[i], 0, 0))],
        out_specs=pl.BlockSpec((MOE_TILE, D), lambda i, te, nt: (i, 0)))
    return pl.pallas_call(
        _ffn_kernel,
        grid_spec=grid_spec,
        out_shape=jax.ShapeDtypeStruct((NP, D), BF16),
        compiler_params=_cparams(("arbitrary",)),
        name="expert_ffn",
    )(tile_expert, n_tiles_used, x_sorted, wg, wu, wd)


def _dispatch(e_idx, n_tok):
    n_asg = 2 * n_tok
    e_all = e_idx.reshape(-1)
    order = jnp.argsort(e_all, stable=True).astype(I32)
    counts = jnp.sum(e_all[:, None] == jnp.arange(N_EXPERTS, dtype=I32)[None, :], axis=0, dtype=I32)
    starts = jnp.cumsum(counts) - counts
    pcounts = (counts + MOE_TILE - 1) // MOE_TILE * MOE_TILE
    pends = jnp.cumsum(pcounts)
    pstarts = pends - pcounts
    n_tiles = (n_asg + MOE_TILE - 1) // MOE_TILE + N_EXPERTS
    tile_expert = jnp.minimum(jnp.searchsorted(pends, jnp.arange(n_tiles, dtype=I32) * MOE_TILE, side="right"),
                              N_EXPERTS - 1).astype(I32)
    rows = jnp.arange(n_tiles * MOE_TILE, dtype=I32)
    re = tile_expert[rows // MOE_TILE]
    local = rows - pstarts[re]
    valid = local < counts[re]
    src = jnp.where(valid, order[jnp.clip(starts[re] + local, 0, n_asg - 1)], 0)
    row_token = src % n_tok
    rank = jnp.zeros((n_asg,), I32).at[order].set(jnp.arange(n_asg, dtype=I32))
    dest = pstarts[e_all] + rank - starts[e_all]
    n_used = (pends[-1] // MOE_TILE).astype(I32).reshape(1)
    return row_token, tile_expert, n_used, dest.reshape(2, n_tok)


def _combine_kernel(x_ref, y0_ref, y1_ref, rw_ref, g2_ref, lng_ref, lnb_ref, o_ref, *, alpha):
    rw = rw_ref[...]
    y = rw[:, 0:1] * y0_ref[...].astype(F32) + rw[:, 1:2] * y1_ref[...].astype(F32)
    o_ref[...] = _layernorm(alpha * x_ref[...] + g2_ref[...] * y, lng_ref[...], lnb_ref[...])


def _combine(x1, y0, y1, rw, mod1, lng, lnb, alpha, tm):
    G, R, D = x1.shape
    Rm = mod1.shape[1]
    tmm = tm if Rm > 1 else 1
    row = lambda g, i: (g, i, 0)
    const = lambda g, i: (0, 0)
    return pl.pallas_call(
        functools.partial(_combine_kernel, alpha=alpha),
        grid=(G, R // tm),
        in_specs=[pl.BlockSpec((None, tm, D), row), pl.BlockSpec((None, tm, D), row),
                  pl.BlockSpec((None, tm, D), row), pl.BlockSpec((None, tm, LANES), row),
                  pl.BlockSpec((None, tmm, D), (lambda g, i: (g, i, 2)) if Rm > 1 else (lambda g, i: (g, 0, 2))),
                  pl.BlockSpec((1, D), const), pl.BlockSpec((1, D), const)],
        out_specs=pl.BlockSpec((None, tm, D), row),
        out_shape=jax.ShapeDtypeStruct((G, R, D), F32),
        compiler_params=_cparams(("arbitrary", "arbitrary")),
        name="combine",
    )(x1, y0, y1, rw, mod1, lng, lnb)


def _pad_cols(w, width):
    return jnp.pad(w, ((0, 0), (0, width - w.shape[1])))


def _prep_layer(l, w_in, mla_q_norm, mla_w_uq, mla_kv_norm, mla_w_uk, mla_w_uv, fox_bf, w_branch, w_o,
                ln1_g, ln1_b, moe_w_grp, moe_b_grp, moe_w_exp, moe_b_exp, ln2_g, ln2_b):
    w = w_in[l]
    D = w.shape[0]
    o = [0, 256, 640, 652, 908, 1036, 1068, 1836, 1840, 1840 + N_BRANCH * D]
    w_p = jnp.concatenate([w[:, o[0]:o[2]], _pad_cols(w[:, o[2]:o[3]], LANES), w[:, o[3]:o[5]],
                           _pad_cols(w[:, o[5]:o[6]], LANES), w[:, o[6]:o[7]],
                           _pad_cols(w[:, o[7]:o[8]], LANES), w[:, o[8]:o[9]]], axis=1).astype(BF16)
    uq = mla_w_uq[l].reshape(MLA_Q_LORA, N_HEADS, MLA_NOPE + MLA_ROPE)
    wuq = jnp.concatenate([uq[:, :, :MLA_NOPE].reshape(MLA_Q_LORA, -1),
                           uq[:, :, MLA_NOPE:].reshape(MLA_Q_LORA, -1)], axis=1).astype(BF16)
    uk = mla_w_uk[l]
    wqc = jnp.zeros((N_HEADS * MLA_NOPE + N_HEADS * MLA_ROPE, N_HEADS * MLA_QW), F32)
    eye = jnp.eye(MLA_ROPE, dtype=F32)
    for h in range(N_HEADS):
        wqc = wqc.at[h * MLA_NOPE:(h + 1) * MLA_NOPE, h * MLA_QW:h * MLA_QW + MLA_KV_LORA].set(uk[:, h, :].T)
        r0 = N_HEADS * MLA_NOPE + h * MLA_ROPE
        c0 = h * MLA_QW + MLA_KV_LORA
        wqc = wqc.at[r0:r0 + MLA_ROPE, c0:c0 + MLA_ROPE].set(eye)
    uv = mla_w_uv[l]
    wuv = jnp.zeros((N_HEADS * MLA_KV_LORA, N_HEADS * MLA_V), F32)
    for h in range(N_HEADS):
        wuv = wuv.at[h * MLA_KV_LORA:(h + 1) * MLA_KV_LORA, h * MLA_V:(h + 1) * MLA_V].set(uv[:, h, :])
    fbf = _pad_cols(fox_bf[l][None, :], LANES)
    wr = _pad_cols(jnp.concatenate([moe_w_grp[l], moe_w_exp[l]], axis=1), LANES)
    br = _pad_cols(jnp.concatenate([moe_b_grp[l], moe_b_exp[l]])[None, :], LANES)
    inproj_w = (w_p, mla_q_norm[l][None, :], wuq, wqc.astype(BF16), mla_kv_norm[l][None, :], fbf)
    merge_w = (wuv.astype(BF16), w_branch[l].astype(BF16), w_o[l].astype(BF16),
               ln1_g[l][None, :], ln1_b[l][None, :], wr, br)
    return inproj_w, merge_w, (ln2_g[l][None, :], ln2_b[l][None, :])


def kernel(x_prompt, x_sample, cache_nsa_kv, state_nsa_win, cache_mla_latent, cache_mla_rope, cache_fox_kv, cache_fox_logf, page_table, c_prompt, c_sample, ada_w, ada_b, w_in, nsa_cmp_pos, mla_q_norm, mla_w_uq, mla_kv_norm, mla_w_uk, mla_w_uv, fox_bf, w_branch, w_o, ln1_g, ln1_b, moe_w_grp, moe_b_grp, moe_w_exp, moe_b_exp, moe_w_gate, moe_w_up, moe_w_down, ln2_g, ln2_b):
    B, T, D = x_prompt.shape
    DB = x_sample.shape[0]
    assert x_sample.shape[1] == 1
    depth = ada_w.shape[0]
    n_pool = cache_nsa_kv.shape[1]
    NP = page_table.shape[1]
    past_len = NP * LANES
    assert cache_nsa_kv.shape[2] == LANES
    alpha = (2 * depth) ** 0.25
    tm = min(ROW_TILE, T)
    assert T % tm == 0
    n_keep = min(WINDOW, T)

    c_all = jnp.concatenate([c_prompt, c_sample], axis=0)
    ada = _ada(c_all, ada_w.reshape(depth * 2, D, 3 * D), ada_b.reshape(depth * 2, 3 * D))

    rot_p = _rot_tables(jnp.arange(T, dtype=I32))
    rot_s = _rot_tables(jnp.full((1,), past_len, I32))
    nsa_view = cache_nsa_kv.reshape(depth, n_pool, LANES, 256)
    win_view = state_nsa_win.reshape(depth, DB, state_nsa_win.shape[2], LANES)
    fox_view = cache_fox_kv.reshape(depth, n_pool, LANES, 512)
    lf_t = jnp.pad(jnp.swapaxes(cache_fox_logf, 2, 3), ((0, 0), (0, 0), (0, SUBLANES - N_HEADS), (0, 0)))
    cmp_page = None

    y_p = x_prompt
    y_s = x_sample.reshape(1, DB, D)
    outs_p = [[] for _ in range(6)]
    outs_s = [[] for _ in range(6)]
    head_pad = ((0, 0), (0, SUBLANES - N_HEADS), (0, 0))
    for l in range(depth):
        inproj_w, merge_w, (ln2g, ln2b) = _prep_layer(
            l, w_in, mla_q_norm, mla_w_uq, mla_kv_norm, mla_w_uk, mla_w_uv, fox_bf, w_branch, w_o,
            ln1_g, ln1_b, moe_w_grp, moe_b_grp, moe_w_exp, moe_b_exp, ln2_g, ln2_b)
        mod0_p = ada[2 * l, :B].reshape(B, 1, 3 * D)
        mod1_p = ada[2 * l + 1, :B].reshape(B, 1, 3 * D)
        mod0_s = ada[2 * l, B:].reshape(1, DB, 3 * D)
        mod1_s = ada[2 * l + 1, B:].reshape(1, DB, 3 * D)
        cmp_t = jnp.tile(nsa_cmp_pos[l], (1, T // CMP_BLOCK))
        cmp_page = jnp.tile(nsa_cmp_pos[l], (1, LANES // CMP_BLOCK))

        (qn, rows, win, selwin, gates, mq, lat, rope, mkey, fq, frows, fkv, logf, bg) = _inproj(
            y_p, mod0_p, rot_p, inproj_w, tm)
        fcol, frow = _fox_cumsum(logf, tm)
        o_nsa = _nsa_prompt(qn, rows, selwin, gates, cmp_t, tm)
        o_lat = _mla_prompt(mq, mkey, tm)
        o_fox = _fox_prompt(fq, fkv, fcol, frow, tm)
        x1_p, h2_p, ri_p, rw_p = _merge(y_p, o_nsa, o_lat, o_fox, bg, mod0_p, mod1_p, merge_w, alpha, tm)
        outs_p[0].append(rows.reshape(B, T, 4, HEAD_DIM))
        outs_p[1].append(win[:, T - n_keep:].reshape(B, n_keep, 2, HEAD_DIM))
        outs_p[2].append(lat)
        outs_p[3].append(rope)
        outs_p[4].append(frows.reshape(B, T, 2, N_HEADS, HEAD_DIM))
        outs_p[5].append(logf[:, :, :N_HEADS])

        (qn, rows, win, selwin, gates, mq, lat, rope, mkey, fq, frows, fkv, logf, bg) = _inproj(
            y_s, mod0_s, rot_s, inproj_w, DB)
        qn_h = jnp.pad(qn[0].astype(F32).reshape(DB, N_HEADS, HEAD_DIM), head_pad)
        sel, o_c = _nsa_cmp_decode(l, page_table, qn_h, cmp_page, nsa_view, past_len)
        new_sw = jnp.concatenate([rows[0][:, LANES:], win[0]], axis=-1).reshape(DB, 1, 256)
        gates_t = jnp.pad(jnp.swapaxes(gates[0][:, :N_BRANCH * N_HEADS].reshape(DB, N_BRANCH, N_HEADS), 1, 2),
                          ((0, 0), (0, SUBLANES - N_HEADS), (0, LANES - N_BRANCH)))
        o_nsa_s = _nsa_sel_decode(l, page_table, sel, qn_h, o_c, new_sw, gates_t, nsa_view, win_view)
        mq_h = jnp.pad(mq[0].astype(F32).reshape(DB, N_HEADS, MLA_QW), head_pad)
        o_lat_s = _mla_decode(l, page_table, mq_h, mkey[0].astype(F32).reshape(DB, 1, MLA_QW),
                              cache_mla_latent, cache_mla_rope)
        fq_h = fq[0].astype(F32).reshape(DB, N_HEADS, 1, HEAD_DIM)
        fq_bd = (fq_h * jnp.eye(N_HEADS, dtype=F32)[None, :, :, None]).reshape(DB, N_HEADS, 256)
        lf_new = jnp.pad(logf[0][:, :N_HEADS, None], head_pad)
        o_fox_s = _fox_decode(l, page_table, jnp.pad(fq_bd, head_pad), frows[0].reshape(DB, 1, 512), lf_new,
                              fox_view, lf_t)
        x1_s, h2_s, ri_s, rw_s = _merge(
            y_s, o_nsa_s[:, :N_HEADS].reshape(1, DB, 256).astype(BF16),
            o_lat_s[:, :N_HEADS].reshape(1, DB, N_HEADS * MLA_KV_LORA).astype(BF16),
            o_fox_s.reshape(1, DB, 256).astype(BF16), bg, mod0_s, mod1_s, merge_w, alpha, DB)
        outs_s[0].append(rows[0].reshape(DB, 1, 4, HEAD_DIM))
        outs_s[1].append(jnp.concatenate([state_nsa_win[l][:, 1:], win[0].reshape(DB, 1, 2, HEAD_DIM)], axis=1))
        outs_s[2].append(lat[0].reshape(DB, 1, MLA_KV_LORA))
        outs_s[3].append(rope[0].reshape(DB, 1, MLA_ROPE))
        outs_s[4].append(frows[0].reshape(DB, 1, 2, N_HEADS, HEAD_DIM))
        outs_s[5].append(logf[0][:, :N_HEADS].reshape(DB, 1, N_HEADS))

        n_tok = B * T + DB
        h2_all = jnp.concatenate([h2_p.reshape(B * T, D), h2_s.reshape(DB, D)], axis=0)
        e_idx = jnp.concatenate([ri_p.reshape(B * T, LANES)[:, :2], ri_s.reshape(DB, LANES)[:, :2]], axis=0).T
        row_token, tile_expert, n_used, dest = _dispatch(e_idx, n_tok)
        y_sorted = _expert_ffn(jnp.take(h2_all, row_token, axis=0), tile_expert, n_used,
                               moe_w_gate[l].astype(BF16), moe_w_up[l].astype(BF16), moe_w_down[l].astype(BF16))
        y0 = jnp.take(y_sorted, dest[0], axis=0)
        y1 = jnp.take(y_sorted, dest[1], axis=0)
        y_p = _combine(x1_p, y0[:B * T].reshape(B, T, D), y1[:B * T].reshape(B, T, D), rw_p, mod1_p,
                       ln2g, ln2b, alpha, tm)
        y_s = _combine(x1_s, y0[B * T:].reshape(1, DB, D), y1[B * T:].reshape(1, DB, D), rw_s, mod1_s,
                       ln2g, ln2b, alpha, DB)

    return (y_p, y_s.reshape(DB, 1, D),
            *[jnp.stack(a) for a in outs_p], *[jnp.stack(a) for a in outs_s])
```

```python
import functools

import jax
import jax.numpy as jnp
from jax import lax
from jax.experimental import pallas as pl
from jax.experimental.pallas import tpu as pltpu

F32 = jnp.float32
BF16 = jnp.bfloat16
I32 = jnp.int32
HIGHEST = lax.Precision.HIGHEST

HEAD_DIM = 64
N_HEADS = 4
ROT_DIM = HEAD_DIM // 4
ROPE_THETA = 500000.0
CMP_BLOCK = 32
SEL_BLOCK = 64
TOP_N = 8
WINDOW = 512
MLA_Q_LORA = 256
MLA_KV_LORA = 128
MLA_NOPE = 64
MLA_ROPE = 32
MLA_V = 64
BRANCH_W = 256
N_BRANCH = 3
N_GROUPS = 4
EXP_PER_GROUP = 8
N_EXPERTS = N_GROUPS * EXP_PER_GROUP
LN_EPS = 1e-5
NEG_INF = -1e30
MLA_SCALE = (MLA_NOPE + MLA_ROPE) ** -0.5
ATT_SCALE = HEAD_DIM ** -0.5

LANES = 128
SUBLANES = 8
VMEM_LIMIT = 56 * 1024 * 1024
ROW_TILE = 256
MOE_TILE = 256

SEG_Q = 0
SEG_KV = 256
SEG_G = 640
SEG_DQ = 768
SEG_LAT = 1024
SEG_ROPE = 1152
SEG_FQKV = 1280
SEG_FF = 2048
SEG_BG = 2176
MLA_QW = 256


def _cparams(sem):
    return pltpu.CompilerParams(dimension_semantics=sem, vmem_limit_bytes=VMEM_LIMIT)


def _iota(shape, dim, dtype=I32):
    return lax.broadcasted_iota(dtype, shape, dim)


def _nt_dot(a, b):
    return lax.dot_general(a, b, (((1,), (1,)), ((), ())), preferred_element_type=F32)


def _dot(a, b, precision=None):
    return jnp.dot(a, b, preferred_element_type=F32, precision=precision)


def _ada_kernel(c_ref, w_ref, b_ref, o_ref):
    c = c_ref[...]
    s = c * jax.nn.sigmoid(c)
    o_ref[...] = _dot(s, w_ref[...], HIGHEST) + b_ref[...]


def _ada(c_all, ada_w, ada_b):
    S, D, D3 = ada_w.shape
    Bc = c_all.shape[0]
    tn = D3 // 3
    return pl.pallas_call(
        _ada_kernel,
        grid=(S, D3 // tn),
        in_specs=[pl.BlockSpec((Bc, D), lambda s, n: (0, 0)),
                  pl.BlockSpec((None, D, tn), lambda s, n: (s, 0, n)),
                  pl.BlockSpec((None, 1, tn), lambda s, n: (s, 0, n))],
        out_specs=pl.BlockSpec((None, Bc, tn), lambda s, n: (s, 0, n)),
        out_shape=jax.ShapeDtypeStruct((S, Bc, D3), F32),
        compiler_params=_cparams(("arbitrary", "arbitrary")),
        name="ada",
    )(c_all, ada_w, ada_b.reshape(S, 1, D3))


def _rot_tables(pos):
    posf = pos.astype(F32)[:, None]
    lane = jnp.arange(LANES)

    def table(period, half, rot_dim, active):
        i = lane % period
        in_rot = (i < rot_dim) & active
        inv = 1.0 / (ROPE_THETA ** ((i % half).astype(F32) * 2.0 / rot_dim))
        ang = posf * inv[None, :]
        cos = jnp.where(in_rot[None, :], jnp.cos(ang), 1.0)
        sin = jnp.where(in_rot[None, :], jnp.where((i < half)[None, :], -jnp.sin(ang), jnp.sin(ang)), 0.0)
        return [cos, sin]

    h = ROT_DIM // 2
    t = (table(HEAD_DIM, h, ROT_DIM, lane >= 0)
         + table(HEAD_DIM, h, ROT_DIM, (lane % LANES) < HEAD_DIM)
         + table(LANES, MLA_ROPE // 2, MLA_ROPE, lane >= 0)
         + table(MLA_ROPE, MLA_ROPE // 2, MLA_ROPE, lane >= 0))
    return jnp.stack(t).astype(F32)


def _inproj_kernel(x_ref, sh_ref, sc_ref, rot_ref, w_ref, qn_ref, wuq_ref, wqc_ref, kvn_ref, fbf_ref,
                   qn_o, rows_o, win_o, selwin_o, gate_o, mq_o, lat_o, rope_o, mkey_o,
                   fq_o, frows_o, fkv_o, logf_o, bg_o):
    D = x_ref.shape[-1]
    h = x_ref[...] * (1.0 + sc_ref[...]) + sh_ref[...]
    hb = h.astype(BF16)
    lane = _iota((1, LANES), 1)

    def proj(a, width):
        return _dot(hb, w_ref[:, a:a + width])

    def rot(xc, t, half, period):
        cos = rot_ref[2 * t]
        sin = rot_ref[2 * t + 1]
        fwd = pltpu.roll(xc, LANES - half, 1)
        bwd = pltpu.roll(xc, half, 1)
        first = (lane & (period - 1)) < half
        return xc * cos + jnp.where(first, fwd, bwd) * sin

    hr = ROT_DIM // 2
    q = proj(SEG_Q, 256)
    for c in range(2):
        qc = rot(q[:, c * LANES:(c + 1) * LANES], 0, hr, HEAD_DIM)
        qn_o[:, c * LANES:(c + 1) * LANES] = (qc * ATT_SCALE).astype(BF16)
    kv = proj(SEG_KV, 384)
    for c in range(3):
        r = rot(kv[:, c * LANES:(c + 1) * LANES], 1, hr, HEAD_DIM)
        if c < 2:
            rows_o[:, c * LANES:(c + 1) * LANES] = r
        else:
            win_o[...] = r
        if c >= 1:
            selwin_o[:, (c - 1) * LANES:c * LANES] = r.astype(BF16)
    gate_o[...] = jax.nn.sigmoid(proj(SEG_G, LANES))
    dq = proj(SEG_DQ, MLA_Q_LORA)
    dqn = dq * lax.rsqrt(jnp.mean(dq * dq, axis=-1, keepdims=True) + LN_EPS) * qn_ref[...]
    qh = _dot(dqn.astype(BF16), wuq_ref[...])
    qrope = rot(qh[:, 256:384], 3, MLA_ROPE // 2, MLA_ROPE)
    mq = _dot(qh[:, :256].astype(BF16), wqc_ref[0:256, :]) + _dot(qrope.astype(BF16), wqc_ref[256:384, :])
    mq_o[...] = (mq * MLA_SCALE).astype(BF16)
    lat = proj(SEG_LAT, 256)
    latc = lat[:, :LANES]
    latn = latc * lax.rsqrt(jnp.mean(latc * latc, axis=-1, keepdims=True) + LN_EPS) * kvn_ref[...]
    ropec = rot(lat[:, LANES:], 2, MLA_ROPE // 2, MLA_ROPE)
    lat_o[...] = latn
    rope_o[...] = ropec[:, :MLA_ROPE]
    mkey_o[:, :LANES] = latn.astype(BF16)
    mkey_o[:, LANES:] = ropec.astype(BF16)
    fqkv = proj(SEG_FQKV, 768)
    fq_o[...] = (fqkv[:, :256] * ATT_SCALE).astype(BF16)
    frows_o[...] = fqkv[:, 256:]
    fkv_o[...] = fqkv[:, 256:].astype(BF16)
    z = proj(SEG_FF, LANES) + fbf_ref[...]
    logf_o[...] = jnp.minimum(z, 0.0) - jnp.log(1.0 + jnp.exp(-jnp.abs(z)))
    for c in range(N_BRANCH):
        bg_o[:, c * D:(c + 1) * D] = jax.nn.sigmoid(proj(SEG_BG + c * D, D)).astype(BF16)


def _inproj(x, mod, rot, wts, tm):
    G, R, D = x.shape
    Rm = mod.shape[1]
    Rt = rot.shape[1]
    tmm = tm if Rm > 1 else 1
    tmt = tm if Rt > 1 else 1
    w_in, qn, wuq, wqc, kvn, fbf = wts
    widths = [(256, BF16), (256, F32), (128, F32), (256, BF16), (128, F32), (N_HEADS * MLA_QW, BF16),
              (128, F32), (MLA_ROPE, F32), (256, BF16), (256, BF16), (512, F32), (512, BF16), (128, F32),
              (N_BRANCH * D, BF16)]
    const = lambda g, i: (0, 0)
    row = lambda g, i: (g, i, 0)
    return pl.pallas_call(
        _inproj_kernel,
        grid=(G, R // tm),
        in_specs=[pl.BlockSpec((None, tm, D), row),
                  pl.BlockSpec((None, tmm, D), (lambda g, i: (g, i, 0)) if Rm > 1 else (lambda g, i: (g, 0, 0))),
                  pl.BlockSpec((None, tmm, D), (lambda g, i: (g, i, 1)) if Rm > 1 else (lambda g, i: (g, 0, 1))),
                  pl.BlockSpec((8, tmt, LANES), (lambda g, i: (0, i, 0)) if Rt > 1 else (lambda g, i: (0, 0, 0))),
                  pl.BlockSpec(w_in.shape, const),
                  pl.BlockSpec(qn.shape, const),
                  pl.BlockSpec(wuq.shape, const),
                  pl.BlockSpec(wqc.shape, const),
                  pl.BlockSpec(kvn.shape, const),
                  pl.BlockSpec(fbf.shape, const)],
        out_specs=[pl.BlockSpec((None, tm, w), row) for w, _ in widths],
        out_shape=[jax.ShapeDtypeStruct((G, R, w), dt) for w, dt in widths],
        compiler_params=_cparams(("arbitrary", "arbitrary")),
        name="inproj",
    )(x, mod, mod, rot, w_in, qn, wuq, wqc, kvn, fbf)


def _cumsum_kernel(lf_ref, fcol_ref, frow_ref, carry):
    tc = lf_ref.shape[0]

    @pl.when(pl.program_id(1) == 0)
    def _():
        carry[...] = jnp.zeros_like(carry)

    lower = (_iota((tc, tc), 1) <= _iota((tc, tc), 0)).astype(F32)
    f = _dot(lower, lf_ref[...], HIGHEST) + carry[...]
    fcol_ref[...] = f
    carry[...] = f[tc - 1:tc, :]
    frow_ref[...] = f.T[0:SUBLANES, :]


def _fox_cumsum(logf, tc):
    B, T, _ = logf.shape
    return pl.pallas_call(
        _cumsum_kernel,
        grid=(B, T // tc),
        in_specs=[pl.BlockSpec((None, tc, LANES), lambda b, i: (b, i, 0))],
        out_specs=[pl.BlockSpec((None, tc, LANES), lambda b, i: (b, i, 0)),
                   pl.BlockSpec((None, SUBLANES, tc), lambda b, i: (b, 0, i))],
        out_shape=[jax.ShapeDtypeStruct((B, T, LANES), F32),
                   jax.ShapeDtypeStruct((B, SUBLANES, T), F32)],
        scratch_shapes=[pltpu.VMEM((1, LANES), F32)],
        compiler_params=_cparams(("arbitrary", "arbitrary")),
        name="fox_cumsum",
    )(logf)


def _online_step(s, mask, v, m, l, acc):
    if mask is not None:
        s = jnp.where(mask, s, NEG_INF)
    m_new = jnp.maximum(m, jnp.max(s, axis=-1, keepdims=True))
    alpha = jnp.exp(m - m_new)
    p = jnp.exp(s - m_new)
    if mask is not None:
        p = jnp.where(mask, p, 0.0)
    l = alpha * l + jnp.sum(p, axis=-1, keepdims=True)
    acc = alpha * acc + _dot(p.astype(v.dtype), v)
    return m_new, l, acc


def _normalize(acc, l):
    return jnp.where(l > 0.0, acc / jnp.where(l > 0.0, l, 1.0), 0.0)


def _fox_kernel(q_ref, kv_ref, fcol_ref, frow_ref, o_ref, *, tk):
    tq = q_ref.shape[0]
    i = pl.program_id(1)
    qpos = i * tq + _iota((tq, 1), 0)
    assert tq == tk
    fq_all = fcol_ref[...]
    qs = [q_ref[:, h * HEAD_DIM:(h + 1) * HEAD_DIM] for h in range(N_HEADS)]
    fqs = [fq_all[:, h:h + 1] for h in range(N_HEADS)]

    def tile(j, carry, diagonal):
        ks = pl.multiple_of(j * tk, tk)
        mask = ((ks + _iota((1, tk), 1)) <= qpos) if diagonal else None
        out = []
        for h in range(N_HEADS):
            lo = h * HEAD_DIM
            k = kv_ref[pl.ds(ks, tk), lo:lo + HEAD_DIM]
            v = kv_ref[pl.ds(ks, tk), 256 + lo:256 + lo + HEAD_DIM]
            s = _nt_dot(qs[h], k) + (fqs[h] - frow_ref[h:h + 1, pl.ds(ks, tk)])
            out.append(_online_step(s, mask, v, *carry[h]))
        return tuple(out)

    init = tuple((jnp.full((tq, 1), NEG_INF, F32), jnp.zeros((tq, 1), F32), jnp.zeros((tq, HEAD_DIM), F32))
                 for _ in range(N_HEADS))
    st = lax.fori_loop(0, i, lambda j, c: tile(j, c, False), init)
    st = tile(i, st, True)
    o_ref[...] = jnp.concatenate([_normalize(acc, l) for _, l, acc in st], axis=-1).astype(o_ref.dtype)


def _fox_prompt(q, kv, fcol, frow, tq):
    B, T, _ = q.shape
    return pl.pallas_call(
        functools.partial(_fox_kernel, tk=tq),
        grid=(B, T // tq),
        in_specs=[pl.BlockSpec((None, tq, 256), lambda b, i: (b, i, 0)),
                  pl.BlockSpec((None, T, 512), lambda b, i: (b, 0, 0)),
                  pl.BlockSpec((None, tq, LANES), lambda b, i: (b, i, 0)),
                  pl.BlockSpec((None, SUBLANES, T), lambda b, i: (b, 0, 0))],
        out_specs=pl.BlockSpec((None, tq, 256), lambda b, i: (b, i, 0)),
        out_shape=jax.ShapeDtypeStruct((B, T, 256), BF16),
        compiler_params=_cparams(("arbitrary", "arbitrary")),
        name="fox_prompt",
    )(q, kv, fcol, frow)


def _mla_kernel(q_ref, key_ref, o_ref, *, tk):
    tq = q_ref.shape[0]
    i = pl.program_id(1)
    qpos = i * tq + _iota((tq, 1), 0)
    assert tq == tk
    qs = [q_ref[:, h * MLA_QW:(h + 1) * MLA_QW] for h in range(N_HEADS)]

    def tile(j, carry, diagonal):
        ks = pl.multiple_of(j * tk, tk)
        mask = ((ks + _iota((1, tk), 1)) <= qpos) if diagonal else None
        k = key_ref[pl.ds(ks, tk), :]
        v = k[:, :MLA_KV_LORA]
        return tuple(_online_step(_nt_dot(qs[h], k), mask, v, *carry[h]) for h in range(N_HEADS))

    init = tuple((jnp.full((tq, 1), NEG_INF, F32), jnp.zeros((tq, 1), F32), jnp.zeros((tq, MLA_KV_LORA), F32))
                 for _ in range(N_HEADS))
    st = lax.fori_loop(0, i, lambda j, c: tile(j, c, False), init)
    st = tile(i, st, True)
    for h in range(N_HEADS):
        o_ref[:, h * MLA_KV_LORA:(h + 1) * MLA_KV_LORA] = _normalize(st[h][2], st[h][1]).astype(o_ref.dtype)


def _mla_prompt(q, key, tq):
    B, T, _ = q.shape
    return pl.pallas_call(
        functools.partial(_mla_kernel, tk=tq),
        grid=(B, T // tq),
        in_specs=[pl.BlockSpec((None, tq, N_HEADS * MLA_QW), lambda b, i: (b, i, 0)),
                  pl.BlockSpec((None, T, MLA_QW), lambda b, i: (b, 0, 0))],
        out_specs=pl.BlockSpec((None, tq, N_HEADS * MLA_KV_LORA), lambda b, i: (b, i, 0)),
        out_shape=jax.ShapeDtypeStruct((B, T, N_HEADS * MLA_KV_LORA), BF16),
        compiler_params=_cparams(("arbitrary", "arbitrary")),
        name="mla_prompt",
    )(q, key)


def _select_blocks(score, blk_f, n_sel):
    nb = score.shape[-1]
    sel = jnp.zeros(score.shape, jnp.bool_)
    picks = []
    for _ in range(n_sel):
        mx = jnp.max(score, axis=-1, keepdims=True)
        idx = jnp.min(jnp.where(score == mx, blk_f, float(nb)), axis=-1, keepdims=True)
        pick = blk_f == idx
        sel = jnp.logical_or(sel, pick)
        score = jnp.where(pick, -jnp.inf, score)
        picks.append(idx)
    return sel, picks


def _block_scores(imp, blk, qpos):
    cur = qpos >> 6
    valid = blk * SEL_BLOCK <= qpos
    forced = (blk == 0) | (blk == cur) | (blk == cur - 1)
    return jnp.where(forced & valid, 1e6, jnp.where(valid, imp, -1e6))


def _nsa_kernel(q_ref, rows_ref, selwin_ref, gate_ref, cmpt_ref, o_ref, ckv_ref, *, tk):
    tq = q_ref.shape[0]
    T = rows_ref.shape[0]
    nbc = T // CMP_BLOCK
    nbs = T // SEL_BLOCK
    i = pl.program_id(1)

    @pl.when(i == 0)
    def _():
        ct = cmpt_ref[...]
        e = jnp.exp(ct - jnp.max(ct, axis=-1, keepdims=True))
        a = e / (jnp.sum(e, axis=-1, keepdims=True) / float(nbc))
        inblk = (_iota((nbc, T), 1) >> 5) == _iota((nbc, T), 0)
        rows = rows_ref[...]
        ck = _dot(jnp.where(inblk, a[0:1, :], 0.0), rows, HIGHEST)
        cv = _dot(jnp.where(inblk, a[1:2, :], 0.0), rows, HIGHEST)
        ckv_ref[...] = jnp.where(_iota((1, LANES), 1) < HEAD_DIM, ck, cv)

    qpos = i * tq + _iota((tq, 1), 0)
    ckv = ckv_ref[...].astype(BF16)
    ck = ckv[:, :HEAD_DIM]
    cv = ckv[:, HEAD_DIM:]
    cblk = _iota((1, nbc), 1)
    cmask = (cblk + 1) * CMP_BLOCK - 1 <= qpos
    o_c = []
    psum = jnp.zeros((tq, nbc), F32)
    qs = [q_ref[:, h * HEAD_DIM:(h + 1) * HEAD_DIM] for h in range(N_HEADS)]
    for h in range(N_HEADS):
        s = jnp.where(cmask, _nt_dot(qs[h], ck), NEG_INF)
        p = jnp.where(cmask, jnp.exp(s - jnp.max(s, axis=-1, keepdims=True)), 0.0)
        den = jnp.sum(p, axis=-1, keepdims=True)
        p = p * jnp.where(den > 0.0, 1.0 / jnp.where(den > 0.0, den, 1.0), 0.0)
        o_c.append(_dot(p.astype(BF16), cv))
        psum = psum + p
    pair = ((_iota((nbc, nbs), 0) >> 1) == _iota((nbc, nbs), 1)).astype(F32)
    imp = _dot(psum, pair, HIGHEST)
    blk = _iota((1, nbs), 1)
    sel, _ = _select_blocks(_block_scores(imp, blk, qpos), blk.astype(F32), min(TOP_N, nbs))
    sel_b = sel.astype(BF16)

    def init():
        return tuple((jnp.full((tq, 1), NEG_INF, F32), jnp.zeros((tq, 1), F32),
                      jnp.zeros((tq, HEAD_DIM), F32)) for _ in range(N_HEADS))

    def sel_body(j, carry):
        ks = pl.multiple_of(j * tk, tk)
        kpos = ks + _iota((1, tk), 1)
        expand = (((ks + _iota((nbs, tk), 1)) >> 6) == _iota((nbs, tk), 0)).astype(BF16)
        mask = (_dot(sel_b, expand) > 0.5) & (kpos <= qpos)
        k = selwin_ref[pl.ds(ks, tk), 0:HEAD_DIM]
        v = selwin_ref[pl.ds(ks, tk), HEAD_DIM:2 * HEAD_DIM]
        return tuple(_online_step(_nt_dot(qs[h], k), mask, v, *carry[h]) for h in range(N_HEADS))

    n_kv = (i * tq) // tk + tq // tk
    st_s = lax.fori_loop(0, n_kv, sel_body, init())

    def win_body(j, carry):
        ks = pl.multiple_of(j * tk, tk)
        dist = qpos - (ks + _iota((1, tk), 1))
        mask = (dist >= 0) & (dist <= WINDOW)
        k = selwin_ref[pl.ds(ks, tk), 2 * HEAD_DIM:3 * HEAD_DIM]
        v = selwin_ref[pl.ds(ks, tk), 3 * HEAD_DIM:4 * HEAD_DIM]
        return tuple(_online_step(_nt_dot(qs[h], k), mask, v, *carry[h]) for h in range(N_HEADS))

    first = jnp.maximum((i * tq - WINDOW) // tk, 0)
    st_w = lax.fori_loop(first, n_kv, win_body, init())

    g = gate_ref[...]
    outs = []
    for h in range(N_HEADS):
        o_s = _normalize(st_s[h][2], st_s[h][1])
        o_w = _normalize(st_w[h][2], st_w[h][1])
        outs.append(g[:, h:h + 1] * o_c[h] + g[:, N_HEADS + h:N_HEADS + h + 1] * o_s
                    + g[:, 2 * N_HEADS + h:2 * N_HEADS + h + 1] * o_w)
    o_ref[...] = jnp.concatenate(outs, axis=-1).astype(o_ref.dtype)


def _nsa_prompt(q, rows, selwin, gates, cmp_t, tq):
    B, T, _ = q.shape
    assert T % SEL_BLOCK == 0 and T % tq == 0
    return pl.pallas_call(
        functools.partial(_nsa_kernel, tk=tq),
        grid=(B, T // tq),
        in_specs=[pl.BlockSpec((None, tq, 256), lambda b, i: (b, i, 0)),
                  pl.BlockSpec((None, T, LANES), lambda b, i: (b, 0, 0)),
                  pl.BlockSpec((None, T, 256), lambda b, i: (b, 0, 0)),
                  pl.BlockSpec((None, tq, LANES), lambda b, i: (b, i, 0)),
                  pl.BlockSpec((2, T), lambda b, i: (0, 0))],
        out_specs=pl.BlockSpec((None, tq, 256), lambda b, i: (b, i, 0)),
        out_shape=jax.ShapeDtypeStruct((B, T, 256), BF16),
        scratch_shapes=[pltpu.VMEM((T // CMP_BLOCK, LANES), F32)],
        compiler_params=_cparams(("arbitrary", "arbitrary")),
        name="nsa_prompt",
    )(q, rows, selwin, gates, cmp_t)


FOX_PAGES_PER_STEP = 32
MLA_PAGES_PER_STEP = 64


def _pages_per_step(n_pages, cap):
    g = min(n_pages, cap)
    while n_pages % g:
        g -= 1
    return g


def _fox_dec_kernel(pt_ref, q_ref, new_ref, lfn_ref, *refs, G):
    kv_refs = refs[:G]
    lf_refs = refs[G:2 * G]
    o_ref = refs[2 * G]
    m_ref, l_ref, acc_ref, car_ref = refs[2 * G + 1:]
    s_id = pl.program_id(1)
    q = q_ref[...]

    @pl.when(s_id == 0)
    def _():
        new = new_ref[...]
        m_ref[...] = jnp.sum(q * new[:, :256], axis=-1, keepdims=True)
        l_ref[...] = jnp.ones_like(l_ref)
        acc_ref[...] = jnp.broadcast_to(new[:, 256:], acc_ref.shape)
        car_ref[...] = lfn_ref[...]

    qb = q.astype(BF16)
    later = (_iota((LANES, LANES), 0) > _iota((LANES, LANES), 1)).astype(F32)
    lf_all = jnp.concatenate([lf_refs[g][...] for g in range(G)], axis=0)
    within = _dot(lf_all, later, HIGHEST)
    total = jnp.sum(lf_all, axis=-1, keepdims=True)
    carry = car_ref[...]
    parts = []
    for g in range(G):
        rows = slice(g * SUBLANES, (g + 1) * SUBLANES)
        parts.append(_dot(qb, kv_refs[g][0].astype(BF16)) + within[rows] + carry)
        carry = carry + total[rows]
    car_ref[...] = carry
    s = jnp.concatenate(parts, axis=-1)
    m_old = m_ref[...]
    m_new = jnp.maximum(m_old, jnp.max(s, axis=-1, keepdims=True))
    alpha = jnp.exp(m_old - m_new)
    p = jnp.exp(s - m_new)
    l_ref[...] = alpha * l_ref[...] + jnp.sum(p, axis=-1, keepdims=True)
    m_ref[...] = m_new
    pb = p.astype(BF16)
    pv = None
    for g in range(G):
        t = _nt_dot(pb[:, g * LANES:(g + 1) * LANES], kv_refs[g][1].astype(BF16))
        pv = t if pv is None else pv + t
    acc_ref[...] = alpha * acc_ref[...] + pv

    @pl.when(s_id == pl.num_programs(1) - 1)
    def _():
        o = acc_ref[...] / l_ref[...]
        own = (_iota(o.shape, 1) >> 6) == _iota(o.shape, 0)
        o_ref[...] = jnp.sum(jnp.where(own, o, 0.0), axis=0, keepdims=True)


def _fox_decode(layer, page_table, q_bd, new_row, lf_new, cache_kv, cache_lf_t):
    DB, NP = page_table.shape
    G = _pages_per_step(NP, FOX_PAGES_PER_STEP)
    pt = page_table.reshape(-1)

    def page(g, ndim):
        return lambda b, s, pt: (layer, pt[b * NP + NP - 1 - (s * G + g)]) + (0,) * ndim

    grid_spec = pltpu.PrefetchScalarGridSpec(
        num_scalar_prefetch=1,
        grid=(DB, NP // G),
        in_specs=([pl.BlockSpec((None, SUBLANES, 256), lambda b, s, pt: (b, 0, 0)),
                   pl.BlockSpec((None, 1, 512), lambda b, s, pt: (b, 0, 0)),
                   pl.BlockSpec((None, SUBLANES, 1), lambda b, s, pt: (b, 0, 0))]
                  + [pl.BlockSpec((None, None, 2, 256, LANES), page(g, 3)) for g in range(G)]
                  + [pl.BlockSpec((None, None, SUBLANES, LANES), page(g, 2)) for g in range(G)]),
        out_specs=pl.BlockSpec((None, 1, 256), lambda b, s, pt: (b, 0, 0)),
        scratch_shapes=[pltpu.VMEM((SUBLANES, 1), F32), pltpu.VMEM((SUBLANES, 1), F32),
                        pltpu.VMEM((SUBLANES, 256), F32), pltpu.VMEM((SUBLANES, 1), F32)])
    return pl.pallas_call(
        functools.partial(_fox_dec_kernel, G=G),
        grid_spec=grid_spec,
        out_shape=jax.ShapeDtypeStruct((DB, 1, 256), F32),
        compiler_params=_cparams(("arbitrary", "arbitrary")),
        name="fox_decode",
    )(pt, q_bd, new_row, lf_new, *([cache_kv] * G), *([cache_lf_t] * G))


def _mla_dec_kernel(pt_ref, q_ref, new_ref, *refs, G):
    lat_refs = refs[:G]
    rope_refs = refs[G:2 * G]
    o_ref = refs[2 * G]
    m_ref, l_ref, acc_ref = refs[2 * G + 1:]
    s_id = pl.program_id(1)
    q = q_ref[...]

    @pl.when(s_id == 0)
    def _():
        new = new_ref[...]
        m_ref[...] = jnp.sum(q * new, axis=-1, keepdims=True)
        l_ref[...] = jnp.ones_like(l_ref)
        acc_ref[...] = jnp.broadcast_to(new[:, :MLA_KV_LORA], acc_ref.shape)

    q_lat = q[:, :MLA_KV_LORA].astype(BF16)
    q_rope = q[:, MLA_KV_LORA:].astype(BF16)
    rope_pad = jnp.zeros((LANES - MLA_ROPE, LANES), F32)
    parts = []
    for g in range(G):
        rope_t = jnp.concatenate([rope_refs[g][...], rope_pad], axis=0).astype(BF16)
        parts.append(_nt_dot(q_lat, lat_refs[g][...].astype(BF16)) + _dot(q_rope, rope_t))
    s = jnp.concatenate(parts, axis=-1)
    m_old = m_ref[...]
    m_new = jnp.maximum(m_old, jnp.max(s, axis=-1, keepdims=True))
    alpha = jnp.exp(m_old - m_new)
    p = jnp.exp(s - m_new)
    l_ref[...] = alpha * l_ref[...] + jnp.sum(p, axis=-1, keepdims=True)
    m_ref[...] = m_new
    pb = p.astype(BF16)
    pv = None
    for g in range(G):
        t = _dot(pb[:, g * LANES:(g + 1) * LANES], lat_refs[g][...].astype(BF16))
        pv = t if pv is None else pv + t
    acc_ref[...] = alpha * acc_ref[...] + pv

    @pl.when(s_id == pl.num_programs(1) - 1)
    def _():
        o_ref[...] = acc_ref[...] / l_ref[...]


def _mla_decode(layer, page_table, q_h, new_key, cache_lat, cache_rope):
    DB, NP = page_table.shape
    G = _pages_per_step(NP, MLA_PAGES_PER_STEP)
    pt = page_table.reshape(-1)

    def page(g):
        return lambda b, s, pt: (layer, pt[b * NP + s * G + g], 0, 0)

    grid_spec = pltpu.PrefetchScalarGridSpec(
        num_scalar_prefetch=1,
        grid=(DB, NP // G),
        in_specs=([pl.BlockSpec((None, SUBLANES, MLA_QW), lambda b, s, pt: (b, 0, 0)),
                   pl.BlockSpec((None, 1, MLA_QW), lambda b, s, pt: (b, 0, 0))]
                  + [pl.BlockSpec((None, None, LANES, MLA_KV_LORA), page(g)) for g in range(G)]
                  + [pl.BlockSpec((None, None, MLA_ROPE, LANES), page(g)) for g in range(G)]),
        out_specs=pl.BlockSpec((None, SUBLANES, MLA_KV_LORA), lambda b, s, pt: (b, 0, 0)),
        scratch_shapes=[pltpu.VMEM((SUBLANES, 1), F32), pltpu.VMEM((SUBLANES, 1), F32),
                        pltpu.VMEM((SUBLANES, MLA_KV_LORA), F32)])
    return pl.pallas_call(
        functools.partial(_mla_dec_kernel, G=G),
        grid_spec=grid_spec,
        out_shape=jax.ShapeDtypeStruct((DB, SUBLANES, MLA_KV_LORA), F32),
        compiler_params=_cparams(("arbitrary", "arbitrary")),
        name="mla_decode",
    )(pt, q_h, new_key, *([cache_lat] * G), *([cache_rope] * G))


def _nsa_cmp_kernel(pt_ref, q_ref, cmpt_ref, *refs, G, past_len):
    page_refs = refs[:G]
    sel_ref, oc_ref = refs[G:G + 2]
    ckv_ref = refs[G + 2]
    per_page = LANES // CMP_BLOCK
    ct = cmpt_ref[...]
    e = jnp.exp(ct - jnp.max(ct, axis=-1, keepdims=True))
    a = e / (jnp.sum(e, axis=-1, keepdims=True) / float(per_page))
    wrow = _iota((4 * per_page, LANES), 0)
    is_v_row = ((wrow >> 2) & 1) == 1
    w = jnp.where((_iota((4 * per_page, LANES), 1) >> 5) == (wrow & 3), jnp.where(is_v_row, a[1:2, :], a[0:1, :]), 0.0)
    w_hi = w.astype(BF16).astype(F32)
    wb = jnp.where(wrow < 2 * per_page, w_hi, w - w_hi).astype(BF16)
    is_k = _iota((1, LANES), 1) < HEAD_DIM
    for g in range(G):
        page = page_refs[g][...].reshape(2 * HEAD_DIM, LANES).astype(BF16)
        r = _nt_dot(wb, page)
        t = r[0:2 * per_page] + r[2 * per_page:]
        ckv_ref[g * per_page:(g + 1) * per_page, :] = jnp.where(is_k, t[0:per_page], t[per_page:])

    nbc = ckv_ref.shape[0]
    nb_past = nbc // (SEL_BLOCK // CMP_BLOCK)
    width = sel_ref.shape[-1]
    qpos = past_len
    ckv = ckv_ref[...].astype(BF16)
    q = q_ref[...].astype(BF16)
    cmask = (_iota((1, nbc), 1) + 1) * CMP_BLOCK - 1 <= qpos
    s = jnp.where(cmask, _nt_dot(q, ckv[:, :HEAD_DIM]), NEG_INF)
    p = jnp.where(cmask, jnp.exp(s - jnp.max(s, axis=-1, keepdims=True)), 0.0)
    den = jnp.sum(p, axis=-1, keepdims=True)
    p = p * jnp.where(den > 0.0, 1.0 / jnp.where(den > 0.0, den, 1.0), 0.0)
    oc_ref[...] = _dot(p.astype(BF16), ckv[:, HEAD_DIM:])
    head = _iota((SUBLANES, 1), 0) < N_HEADS
    psum = jnp.sum(jnp.where(head, p, 0.0), axis=0, keepdims=True)
    pair = ((_iota((nbc, width), 0) >> 1) == _iota((nbc, width), 1)).astype(F32)
    imp = _dot(jnp.broadcast_to(psum, (SUBLANES, nbc)), pair, HIGHEST)[0:1, :]
    blk = _iota((1, width), 1)
    n_sel = min(TOP_N, nb_past + 1)
    _, picks = _select_blocks(_block_scores(imp, blk, qpos), blk.astype(F32), n_sel)
    out = jnp.full((1, width), -1.0, F32)
    for r, idx in enumerate(picks):
        out = jnp.where(blk == r, idx, out)
    sel_ref[...] = out.astype(I32)


def _nsa_cmp_decode(layer, page_table, q_h, cmp_page, cache_t, past_len):
    DB, NP = page_table.shape
    pt = page_table.reshape(-1)
    nbc = NP * (LANES // CMP_BLOCK)
    nb_past = nbc // 2
    width = (nb_past + 1 + LANES - 1) // LANES * LANES

    def page(g):
        return lambda b, pt: (layer, pt[b * NP + g], 0, 0, 0)

    grid_spec = pltpu.PrefetchScalarGridSpec(
        num_scalar_prefetch=1,
        grid=(DB,),
        in_specs=([pl.BlockSpec((None, SUBLANES, HEAD_DIM), lambda b, pt: (b, 0, 0)),
                   pl.BlockSpec((2, LANES), lambda b, pt: (0, 0))]
                  + [pl.BlockSpec((None, None, 2, HEAD_DIM, LANES), page(g)) for g in range(NP)]),
        out_specs=[pl.BlockSpec((None, 1, width), lambda b, pt: (b, 0, 0)),
                   pl.BlockSpec((None, SUBLANES, HEAD_DIM), lambda b, pt: (b, 0, 0))],
        scratch_shapes=[pltpu.VMEM((nbc, LANES), F32)])
    return pl.pallas_call(
        functools.partial(_nsa_cmp_kernel, G=NP, past_len=past_len),
        grid_spec=grid_spec,
        out_shape=[jax.ShapeDtypeStruct((DB, 1, width), I32),
                   jax.ShapeDtypeStruct((DB, SUBLANES, HEAD_DIM), F32)],
        compiler_params=_cparams(("arbitrary",)),
        name="nsa_cmp_decode",
    )(pt, q_h, cmp_page, *([cache_t] * NP))


def _nsa_sel_kernel(pt_ref, sel_ref, q_ref, oc_ref, new_ref, gate_ref, *refs, n_sel, nb_past):
    blk_refs = refs[:n_sel]
    win_ref, o_ref = refs[n_sel:]
    b = pl.program_id(0)
    q = q_ref[...]
    qb = q.astype(BF16)
    new = new_ref[...]
    half_of_lane = _iota((1, LANES), 1) >> 6
    parts, masks = [], []
    n_new = jnp.int32(0)
    for r in range(n_sel):
        blk = sel_ref[b * n_sel + r]
        half = jnp.where((blk >= 0) & (blk < nb_past), blk & 1, -1)
        masks.append(half_of_lane == jnp.full((1, LANES), half, I32))
        parts.append(_dot(qb, blk_refs[r][0].astype(BF16)))
        n_new = n_new + jnp.where(blk == nb_past, 1, 0)
    has_new = jnp.full((1, 1), n_new, I32) > 0

    def attend(s, mask, k_new, v_new, new_ok, values):
        if mask is not None:
            s = jnp.where(mask, s, NEG_INF)
        s_new = jnp.where(new_ok, jnp.sum(q * k_new, axis=-1, keepdims=True), NEG_INF)
        m = jnp.maximum(jnp.max(s, axis=-1, keepdims=True), s_new)
        p = jnp.exp(s - m)
        if mask is not None:
            p = jnp.where(mask, p, 0.0)
        p_new = jnp.where(new_ok, jnp.exp(s_new - m), 0.0)
        l = jnp.sum(p, axis=-1, keepdims=True) + p_new
        return _normalize(values(p.astype(BF16)) + p_new * v_new, l)

    def sel_values(pb):
        pv = None
        for r in range(n_sel):
            t = _nt_dot(pb[:, r * LANES:(r + 1) * LANES], blk_refs[r][1].astype(BF16))
            pv = t if pv is None else pv + t
        return pv

    o_s = attend(jnp.concatenate(parts, axis=-1), jnp.concatenate(masks, axis=-1),
                 new[:, 0:HEAD_DIM], new[:, HEAD_DIM:2 * HEAD_DIM], has_new, sel_values)
    o_w = attend(_dot(qb, win_ref[0].astype(BF16)), None,
                 new[:, 2 * HEAD_DIM:3 * HEAD_DIM], new[:, 3 * HEAD_DIM:], jnp.ones((1, 1), jnp.bool_),
                 lambda pb: _nt_dot(pb, win_ref[1].astype(BF16)))
    g = gate_ref[...]
    o_ref[...] = g[:, 0:1] * oc_ref[...] + g[:, 1:2] * o_s + g[:, 2:3] * o_w


def _nsa_sel_decode(layer, page_table, sel, q_h, o_c, new_row, gates_t, cache_t, win_t):
    DB, NP = page_table.shape
    nb_past = NP * (LANES // SEL_BLOCK)
    n_sel = min(TOP_N, nb_past + 1)
    WB = win_t.shape[-1]
    pt = page_table.reshape(-1)
    sel_flat = sel[:, 0, :n_sel].reshape(-1)

    def block(r):
        def index(b, pt, sel):
            blk = jnp.clip(sel[b * n_sel + r], 0, nb_past - 1)
            return (layer, pt[b * NP + blk // 2], 1, 0, 0)
        return index

    row = lambda b, pt, sel: (b, 0, 0)
    grid_spec = pltpu.PrefetchScalarGridSpec(
        num_scalar_prefetch=2,
        grid=(DB,),
        in_specs=([pl.BlockSpec((None, SUBLANES, HEAD_DIM), row),
                   pl.BlockSpec((None, SUBLANES, HEAD_DIM), row),
                   pl.BlockSpec((None, 1, 256), row),
                   pl.BlockSpec((None, SUBLANES, LANES), row)]
                  + [pl.BlockSpec((None, None, 2, HEAD_DIM, LANES), block(r)) for r in range(n_sel)]
                  + [pl.BlockSpec((None, None, 2, HEAD_DIM, WB), lambda b, pt, sel: (layer, b, 0, 0, 0))]),
        out_specs=pl.BlockSpec((None, SUBLANES, HEAD_DIM), row))
    return pl.pallas_call(
        functools.partial(_nsa_sel_kernel, n_sel=n_sel, nb_past=nb_past),
        grid_spec=grid_spec,
        out_shape=jax.ShapeDtypeStruct((DB, SUBLANES, HEAD_DIM), F32),
        compiler_params=_cparams(("arbitrary",)),
        name="nsa_sel_decode",
    )(pt, sel_flat, q_h, o_c, new_row, gates_t, *([cache_t] * n_sel), win_t)


def _layernorm(y, g, b):
    mu = jnp.mean(y, axis=-1, keepdims=True)
    d = y - mu
    var = jnp.mean(d * d, axis=-1, keepdims=True)
    return d * lax.rsqrt(var + LN_EPS) * g + b


def _merge_kernel(x_ref, onsa_ref, olat_ref, ofox_ref, bg_ref, g1_ref, sh2_ref, sc2_ref,
                  wuv_ref, wb_ref, wo_ref, lng_ref, lnb_ref, wr_ref, br_ref,
                  x1_o, h2_o, ri_o, rw_o, *, alpha):
    D = x_ref.shape[-1]
    o_mla = _dot(olat_ref[...], wuv_ref[...]).astype(BF16)
    branches = (onsa_ref[...], o_mla, ofox_ref[...])
    mix = None
    for n in range(N_BRANCH):
        t = bg_ref[:, n * D:(n + 1) * D].astype(F32) * _dot(branches[n], wb_ref[n])
        mix = t if mix is None else mix + t
    mix = _dot(mix.astype(BF16), wo_ref[...])
    x1 = _layernorm(alpha * x_ref[...] + g1_ref[...] * mix, lng_ref[...], lnb_ref[...])
    x1_o[...] = x1
    h2 = x1 * (1.0 + sc2_ref[...]) + sh2_ref[...]
    h2_o[...] = h2.astype(BF16)
    logits = _dot(h2, wr_ref[...], HIGHEST) + br_ref[...]
    lane = _iota((1, LANES), 1)
    lane_f = lane.astype(F32)
    is_g = lane < N_GROUPS
    gmax = jnp.max(jnp.where(is_g, logits, NEG_INF), axis=-1, keepdims=True)
    gidx = jnp.min(jnp.where(is_g & (logits == gmax), lane_f, float(LANES)), axis=-1, keepdims=True)
    g_w = 1.0 / jnp.sum(jnp.where(is_g, jnp.exp(logits - gmax), 0.0), axis=-1, keepdims=True)
    grp_of_lane = ((lane - N_GROUPS) >> 3).astype(F32)
    in_grp = (lane >= N_GROUPS) & (lane < N_GROUPS + N_EXPERTS) & (grp_of_lane == gidx)
    v1 = jnp.max(jnp.where(in_grp, logits, NEG_INF), axis=-1, keepdims=True)
    i1 = jnp.min(jnp.where(in_grp & (logits == v1), lane_f, float(LANES)), axis=-1, keepdims=True)
    rest = in_grp & (lane_f != i1)
    v2 = jnp.max(jnp.where(rest, logits, NEG_INF), axis=-1, keepdims=True)
    i2 = jnp.min(jnp.where(rest & (logits == v2), lane_f, float(LANES)), axis=-1, keepdims=True)
    e21 = jnp.exp(v2 - v1)
    w1 = g_w / (1.0 + e21)
    w2 = g_w * e21 / (1.0 + e21)
    ri_o[...] = jnp.where(lane == 0, i1 - N_GROUPS, jnp.where(lane == 1, i2 - N_GROUPS, 0.0)).astype(I32)
    rw_o[...] = jnp.where(lane == 0, w1, jnp.where(lane == 1, w2, 0.0))


def _merge(x, o_nsa, o_lat, o_fox, bg, mod0, mod1, wts, alpha, tm):
    G, R, D = x.shape
    Rm = mod0.shape[1]
    tmm = tm if Rm > 1 else 1
    wuv, wb, wo, lng, lnb, wr, br = wts
    row = lambda g, i: (g, i, 0)

    def modspec(j):
        return pl.BlockSpec((None, tmm, D), (lambda g, i: (g, i, j)) if Rm > 1 else (lambda g, i: (g, 0, j)))

    def const(a):
        return pl.BlockSpec(a.shape, lambda g, i: (0,) * a.ndim)

    return pl.pallas_call(
        functools.partial(_merge_kernel, alpha=alpha),
        grid=(G, R // tm),
        in_specs=[pl.BlockSpec((None, tm, D), row),
                  pl.BlockSpec((None, tm, 256), row),
                  pl.BlockSpec((None, tm, N_HEADS * MLA_KV_LORA), row),
                  pl.BlockSpec((None, tm, 256), row),
                  pl.BlockSpec((None, tm, N_BRANCH * D), row),
                  modspec(2), ] + [pl.BlockSpec((None, tmm, D), (lambda g, i: (g, i, 0)) if Rm > 1 else (lambda g, i: (g, 0, 0))),
                                   pl.BlockSpec((None, tmm, D), (lambda g, i: (g, i, 1)) if Rm > 1 else (lambda g, i: (g, 0, 1)))]
                 + [const(a) for a in (wuv, wb, wo, lng, lnb, wr, br)],
        out_specs=[pl.BlockSpec((None, tm, D), row), pl.BlockSpec((None, tm, D), row),
                   pl.BlockSpec((None, tm, LANES), row), pl.BlockSpec((None, tm, LANES), row)],
        out_shape=[jax.ShapeDtypeStruct((G, R, D), F32), jax.ShapeDtypeStruct((G, R, D), BF16),
                   jax.ShapeDtypeStruct((G, R, LANES), I32), jax.ShapeDtypeStruct((G, R, LANES), F32)],
        compiler_params=_cparams(("arbitrary", "arbitrary")),
        name="merge",
    )(x, o_nsa, o_lat, o_fox, bg, mod0, mod1, mod1, wuv, wb, wo, lng, lnb, wr, br)


def _ffn_kernel(te_ref, nt_ref, x_ref, wg_ref, wu_ref, wd_ref, o_ref):
    i = pl.program_id(0)

    @pl.when(i < nt_ref[0])
    def _():
        x = x_ref[...]
        gte = _dot(x, wg_ref[...])
        a = gte * jax.nn.sigmoid(gte) * _dot(x, wu_ref[...])
        o_ref[...] = _dot(a.astype(BF16), wd_ref[...]).astype(o_ref.dtype)

    @pl.when(i >= nt_ref[0])
    def _():
        o_ref[...] = jnp.zeros_like(o_ref)


def _expert_ffn(x_sorted, tile_expert, n_tiles_used, wg, wu, wd):
    NP, D = x_sorted.shape
    E, _, FF = wg.shape
    grid_spec = pltpu.PrefetchScalarGridSpec(
        num_scalar_prefetch=2,
        grid=(NP // MOE_TILE,),
        in_specs=[pl.BlockSpec((MOE_TILE, D), lambda i, te, nt: (i, 0)),
                  pl.BlockSpec((None, D, FF), lambda i, te, nt: (te[i], 0, 0)),
                  pl.BlockSpec((None, D, FF), lambda i, te, nt: (te[i], 0, 0)),
                  pl.BlockSpec((None, FF, D), lambda i, te, nt: (te[i], 0, 0))],
        out_specs=pl.BlockSpec((MOE_TILE, D), lambda i, te, nt: (i, 0)))
    return pl.pallas_call(
        _ffn_kernel,
        grid_spec=grid_spec,
        out_shape=jax.ShapeDtypeStruct((NP, D), BF16),
        compiler_params=_cparams(("arbitrary",)),
        name="expert_ffn",
    )(tile_expert, n_tiles_used, x_sorted, wg, wu, wd)


def _dispatch(e_idx, n_tok):
    n_asg = 2 * n_tok
    e_all = e_idx.reshape(-1)
    order = jnp.argsort(e_all, stable=True).astype(I32)
    counts = jnp.sum(e_all[:, None] == jnp.arange(N_EXPERTS, dtype=I32)[None, :], axis=0, dtype=I32)
    starts = jnp.cumsum(counts) - counts
    pcounts = (counts + MOE_TILE - 1) // MOE_TILE * MOE_TILE
    pends = jnp.cumsum(pcounts)
    pstarts = pends - pcounts
    n_tiles = (n_asg + MOE_TILE - 1) // MOE_TILE + N_EXPERTS
    tile_start = jnp.arange(n_tiles, dtype=I32) * MOE_TILE
    tile_expert = jnp.minimum(jnp.sum(tile_start[:, None] >= pends[None, :], axis=1, dtype=I32), N_EXPERTS - 1)
    rows = jnp.arange(n_tiles * MOE_TILE, dtype=I32)
    re = tile_expert[rows // MOE_TILE]
    local = rows - pstarts[re]
    valid = local < counts[re]
    src = jnp.where(valid, order[jnp.clip(starts[re] + local, 0, n_asg - 1)], 0)
    row_token = src % n_tok
    rank = jnp.zeros((n_asg,), I32).at[order].set(jnp.arange(n_asg, dtype=I32))
    dest = pstarts[e_all] + rank - starts[e_all]
    n_used = (pends[-1] // MOE_TILE).astype(I32).reshape(1)
    return row_token, tile_expert, n_used, dest.reshape(2, n_tok)


def _combine_kernel(x_ref, y0_ref, y1_ref, rw_ref, g2_ref, lng_ref, lnb_ref, o_ref, *, alpha):
    rw = rw_ref[...]
    y = rw[:, 0:1] * y0_ref[...].astype(F32) + rw[:, 1:2] * y1_ref[...].astype(F32)
    o_ref[...] = _layernorm(alpha * x_ref[...] + g2_ref[...] * y, lng_ref[...], lnb_ref[...])


def _combine(x1, y0, y1, rw, mod1, lng, lnb, alpha, tm):
    G, R, D = x1.shape
    Rm = mod1.shape[1]
    tmm = tm if Rm > 1 else 1
    row = lambda g, i: (g, i, 0)
    const = lambda g, i: (0, 0)
    return pl.pallas_call(
        functools.partial(_combine_kernel, alpha=alpha),
        grid=(G, R // tm),
        in_specs=[pl.BlockSpec((None, tm, D), row), pl.BlockSpec((None, tm, D), row),
                  pl.BlockSpec((None, tm, D), row), pl.BlockSpec((None, tm, LANES), row),
                  pl.BlockSpec((None, tmm, D), (lambda g, i: (g, i, 2)) if Rm > 1 else (lambda g, i: (g, 0, 2))),
                  pl.BlockSpec((1, D), const), pl.BlockSpec((1, D), const)],
        out_specs=pl.BlockSpec((None, tm, D), row),
        out_shape=jax.ShapeDtypeStruct((G, R, D), F32),
        compiler_params=_cparams(("arbitrary", "arbitrary")),
        name="combine",
    )(x1, y0, y1, rw, mod1, lng, lnb)


def _pad_cols(w, width):
    return jnp.pad(w, ((0, 0), (0, width - w.shape[1])))


def _prep_layer(l, w_in, mla_q_norm, mla_w_uq, mla_kv_norm, mla_w_uk, mla_w_uv, fox_bf, w_branch, w_o,
                ln1_g, ln1_b, moe_w_grp, moe_b_grp, moe_w_exp, moe_b_exp, ln2_g, ln2_b):
    w = w_in[l]
    D = w.shape[0]
    o = [0, 256, 640, 652, 908, 1036, 1068, 1836, 1840, 1840 + N_BRANCH * D]
    w_p = jnp.concatenate([w[:, o[0]:o[2]], _pad_cols(w[:, o[2]:o[3]], LANES), w[:, o[3]:o[5]],
                           _pad_cols(w[:, o[5]:o[6]], LANES), w[:, o[6]:o[7]],
                           _pad_cols(w[:, o[7]:o[8]], LANES), w[:, o[8]:o[9]]], axis=1).astype(BF16)
    uq = mla_w_uq[l].reshape(MLA_Q_LORA, N_HEADS, MLA_NOPE + MLA_ROPE)
    wuq = jnp.concatenate([uq[:, :, :MLA_NOPE].reshape(MLA_Q_LORA, -1),
                           uq[:, :, MLA_NOPE:].reshape(MLA_Q_LORA, -1)], axis=1).astype(BF16)
    uk = mla_w_uk[l]
    wqc = jnp.zeros((N_HEADS * MLA_NOPE + N_HEADS * MLA_ROPE, N_HEADS * MLA_QW), F32)
    eye = jnp.eye(MLA_ROPE, dtype=F32)
    for h in range(N_HEADS):
        wqc = wqc.at[h * MLA_NOPE:(h + 1) * MLA_NOPE, h * MLA_QW:h * MLA_QW + MLA_KV_LORA].set(uk[:, h, :].T)
        r0 = N_HEADS * MLA_NOPE + h * MLA_ROPE
        c0 = h * MLA_QW + MLA_KV_LORA
        wqc = wqc.at[r0:r0 + MLA_ROPE, c0:c0 + MLA_ROPE].set(eye)
    uv = mla_w_uv[l]
    wuv = jnp.zeros((N_HEADS * MLA_KV_LORA, N_HEADS * MLA_V), F32)
    for h in range(N_HEADS):
        wuv = wuv.at[h * MLA_KV_LORA:(h + 1) * MLA_KV_LORA, h * MLA_V:(h + 1) * MLA_V].set(uv[:, h, :])
    fbf = _pad_cols(fox_bf[l][None, :], LANES)
    wr = _pad_cols(jnp.concatenate([moe_w_grp[l], moe_w_exp[l]], axis=1), LANES)
    br = _pad_cols(jnp.concatenate([moe_b_grp[l], moe_b_exp[l]])[None, :], LANES)
    inproj_w = (w_p, mla_q_norm[l][None, :], wuq, wqc.astype(BF16), mla_kv_norm[l][None, :], fbf)
    merge_w = (wuv.astype(BF16), w_branch[l].astype(BF16), w_o[l].astype(BF16),
               ln1_g[l][None, :], ln1_b[l][None, :], wr, br)
    return inproj_w, merge_w, (ln2_g[l][None, :], ln2_b[l][None, :])


def kernel(x_prompt, x_sample, cache_nsa_kv, state_nsa_win, cache_mla_latent, cache_mla_rope, cache_fox_kv, cache_fox_logf, page_table, c_prompt, c_sample, ada_w, ada_b, w_in, nsa_cmp_pos, mla_q_norm, mla_w_uq, mla_kv_norm, mla_w_uk, mla_w_uv, fox_bf, w_branch, w_o, ln1_g, ln1_b, moe_w_grp, moe_b_grp, moe_w_exp, moe_b_exp, moe_w_gate, moe_w_up, moe_w_down, ln2_g, ln2_b):
    B, T, D = x_prompt.shape
    DB = x_sample.shape[0]
    assert x_sample.shape[1] == 1
    depth = ada_w.shape[0]
    n_pool = cache_nsa_kv.shape[1]
    NP = page_table.shape[1]
    past_len = NP * LANES
    assert cache_nsa_kv.shape[2] == LANES
    alpha = (2 * depth) ** 0.25
    tm = min(ROW_TILE, T)
    assert T % tm == 0
    n_keep = min(WINDOW, T)

    c_all = jnp.concatenate([c_prompt, c_sample], axis=0)
    ada = _ada(c_all, ada_w.reshape(depth * 2, D, 3 * D), ada_b.reshape(depth * 2, 3 * D))

    rot_p = _rot_tables(jnp.arange(T, dtype=I32))
    rot_s = _rot_tables(jnp.full((1,), past_len, I32))
    nsa_t = jnp.transpose(cache_nsa_kv, (0, 1, 3, 4, 2))
    win_t = jnp.transpose(state_nsa_win, (0, 1, 3, 4, 2))
    fox_t = jnp.transpose(cache_fox_kv, (0, 1, 3, 4, 5, 2)).reshape(depth, n_pool, 2, 256, LANES)
    rope_t = jnp.transpose(cache_mla_rope, (0, 1, 3, 2))
    lf_t = jnp.pad(jnp.swapaxes(cache_fox_logf, 2, 3), ((0, 0), (0, 0), (0, SUBLANES - N_HEADS), (0, 0)))

    y_p = x_prompt
    y_s = x_sample.reshape(1, DB, D)
    outs_p = [[] for _ in range(6)]
    outs_s = [[] for _ in range(6)]
    head_pad = ((0, 0), (0, SUBLANES - N_HEADS), (0, 0))
    for l in range(depth):
        inproj_w, merge_w, (ln2g, ln2b) = _prep_layer(
            l, w_in, mla_q_norm, mla_w_uq, mla_kv_norm, mla_w_uk, mla_w_uv, fox_bf, w_branch, w_o,
            ln1_g, ln1_b, moe_w_grp, moe_b_grp, moe_w_exp, moe_b_exp, ln2_g, ln2_b)
        mod0_p = ada[2 * l, :B].reshape(B, 1, 3 * D)
        mod1_p = ada[2 * l + 1, :B].reshape(B, 1, 3 * D)
        mod0_s = ada[2 * l, B:].reshape(1, DB, 3 * D)
        mod1_s = ada[2 * l + 1, B:].reshape(1, DB, 3 * D)
        cmp_t = jnp.tile(nsa_cmp_pos[l], (1, T // CMP_BLOCK))
        cmp_page = jnp.tile(nsa_cmp_pos[l], (1, LANES // CMP_BLOCK))

        (qn, rows, win, selwin, gates, mq, lat, rope, mkey, fq, frows, fkv, logf, bg) = _inproj(
            y_p, mod0_p, rot_p, inproj_w, tm)
        fcol, frow = _fox_cumsum(logf, tm)
        o_nsa = _nsa_prompt(qn, rows, selwin, gates, cmp_t, tm)
        o_lat = _mla_prompt(mq, mkey, tm)
        o_fox = _fox_prompt(fq, fkv, fcol, frow, tm)
        x1_p, h2_p, ri_p, rw_p = _merge(y_p, o_nsa, o_lat, o_fox, bg, mod0_p, mod1_p, merge_w, alpha, tm)
        outs_p[0].append(rows.reshape(B, T, 4, HEAD_DIM))
        outs_p[1].append(win[:, T - n_keep:].reshape(B, n_keep, 2, HEAD_DIM))
        outs_p[2].append(lat)
        outs_p[3].append(rope)
        outs_p[4].append(frows.reshape(B, T, 2, N_HEADS, HEAD_DIM))
        outs_p[5].append(logf[:, :, :N_HEADS])

        (qn, rows, win, selwin, gates, mq, lat, rope, mkey, fq, frows, fkv, logf, bg) = _inproj(
            y_s, mod0_s, rot_s, inproj_w, DB)
        qn_h = jnp.pad(qn[0].astype(F32).reshape(DB, N_HEADS, HEAD_DIM), head_pad)
        sel, o_c = _nsa_cmp_decode(l, page_table, qn_h, cmp_page, nsa_t, past_len)
        new_sw = jnp.concatenate([rows[0][:, LANES:], win[0]], axis=-1).reshape(DB, 1, 256)
        gates_t = jnp.pad(jnp.swapaxes(gates[0][:, :N_BRANCH * N_HEADS].reshape(DB, N_BRANCH, N_HEADS), 1, 2),
                          ((0, 0), (0, SUBLANES - N_HEADS), (0, LANES - N_BRANCH)))
        o_nsa_s = _nsa_sel_decode(l, page_table, sel, qn_h, o_c, new_sw, gates_t, nsa_t, win_t)
        mq_h = jnp.pad(mq[0].astype(F32).reshape(DB, N_HEADS, MLA_QW), head_pad)
        o_lat_s = _mla_decode(l, page_table, mq_h, mkey[0].astype(F32).reshape(DB, 1, MLA_QW),
                              cache_mla_latent, rope_t)
        fq_h = fq[0].astype(F32).reshape(DB, N_HEADS, 1, HEAD_DIM)
        fq_bd = (fq_h * jnp.eye(N_HEADS, dtype=F32)[None, :, :, None]).reshape(DB, N_HEADS, 256)
        lf_new = jnp.pad(logf[0][:, :N_HEADS, None], head_pad)
        o_fox_s = _fox_decode(l, page_table, jnp.pad(fq_bd, head_pad), frows[0].reshape(DB, 1, 512), lf_new,
                              fox_t, lf_t)
        x1_s, h2_s, ri_s, rw_s = _merge(
            y_s, o_nsa_s[:, :N_HEADS].reshape(1, DB, 256).astype(BF16),
            o_lat_s[:, :N_HEADS].reshape(1, DB, N_HEADS * MLA_KV_LORA).astype(BF16),
            o_fox_s.reshape(1, DB, 256).astype(BF16), bg, mod0_s, mod1_s, merge_w, alpha, DB)
        outs_s[0].append(rows[0].reshape(DB, 1, 4, HEAD_DIM))
        outs_s[1].append(jnp.concatenate([state_nsa_win[l][:, 1:], win[0].reshape(DB, 1, 2, HEAD_DIM)], axis=1))
        outs_s[2].append(lat[0].reshape(DB, 1, MLA_KV_LORA))
        outs_s[3].append(rope[0].reshape(DB, 1, MLA_ROPE))
        outs_s[4].append(frows[0].reshape(DB, 1, 2, N_HEADS, HEAD_DIM))
        outs_s[5].append(logf[0][:, :N_HEADS].reshape(DB, 1, N_HEADS))

        n_tok = B * T + DB
        h2_all = jnp.concatenate([h2_p.reshape(B * T, D), h2_s.reshape(DB, D)], axis=0)
        e_idx = jnp.concatenate([ri_p.reshape(B * T, LANES)[:, :2], ri_s.reshape(DB, LANES)[:, :2]], axis=0).T
        row_token, tile_expert, n_used, dest = _dispatch(e_idx, n_tok)
        y_sorted = _expert_ffn(jnp.take(h2_all, row_token, axis=0), tile_expert, n_used,
                               moe_w_gate[l].astype(BF16), moe_w_up[l].astype(BF16), moe_w_down[l].astype(BF16))
        y0 = jnp.take(y_sorted, dest[0], axis=0)
        y1 = jnp.take(y_sorted, dest[1], axis=0)
        y_p = _combine(x1_p, y0[:B * T].reshape(B, T, D), y1[:B * T].reshape(B, T, D), rw_p, mod1_p,
                       ln2g, ln2b, alpha, tm)
        y_s = _combine(x1_s, y0[B * T:].reshape(1, DB, D), y1[B * T:].reshape(1, DB, D), rw_s, mod1_s,
                       ln2g, ln2b, alpha, DB)

    return (y_p, y_s.reshape(DB, 1, D),
            *[jnp.stack(a) for a in outs_p], *[jnp.stack(a) for a in outs_s])
```

```python
import functools

import jax
import jax.numpy as jnp
from jax import lax
from jax.experimental import pallas as pl
from jax.experimental.pallas import tpu as pltpu

F32 = jnp.float32
BF16 = jnp.bfloat16
I32 = jnp.int32
HIGHEST = lax.Precision.HIGHEST

HEAD_DIM = 64
N_HEADS = 4
ROT_DIM = HEAD_DIM // 4
ROPE_THETA = 500000.0
CMP_BLOCK = 32
SEL_BLOCK = 64
TOP_N = 8
WINDOW = 512
MLA_Q_LORA = 256
MLA_KV_LORA = 128
MLA_NOPE = 64
MLA_ROPE = 32
MLA_V = 64
BRANCH_W = 256
N_BRANCH = 3
N_GROUPS = 4
EXP_PER_GROUP = 8
N_EXPERTS = N_GROUPS * EXP_PER_GROUP
LN_EPS = 1e-5
NEG_INF = -1e30
MLA_SCALE = (MLA_NOPE + MLA_ROPE) ** -0.5
ATT_SCALE = HEAD_DIM ** -0.5

LANES = 128
SUBLANES = 8
VMEM_LIMIT = 56 * 1024 * 1024
ROW_TILE = 256
MOE_TILE = 256

SEG_Q = 0
SEG_KV = 256
SEG_G = 640
SEG_DQ = 768
SEG_LAT = 1024
SEG_ROPE = 1152
SEG_FQKV = 1280
SEG_FF = 2048
SEG_BG = 2176
MLA_QW = 256


def _cparams(sem):
    return pltpu.CompilerParams(dimension_semantics=sem, vmem_limit_bytes=VMEM_LIMIT)


def _iota(shape, dim, dtype=I32):
    return lax.broadcasted_iota(dtype, shape, dim)


def _nt_dot(a, b):
    return lax.dot_general(a, b, (((1,), (1,)), ((), ())), preferred_element_type=F32)


def _dot(a, b, precision=None):
    return jnp.dot(a, b, preferred_element_type=F32, precision=precision)


def _ada_kernel(c_ref, w_ref, b_ref, o_ref):
    c = c_ref[...]
    s = c * jax.nn.sigmoid(c)
    o_ref[...] = _dot(s, w_ref[...], HIGHEST) + b_ref[...]


def _ada(c_all, ada_w, ada_b):
    S, D, D3 = ada_w.shape
    Bc = c_all.shape[0]
    tn = D3 // 3
    return pl.pallas_call(
        _ada_kernel,
        grid=(S, D3 // tn),
        in_specs=[pl.BlockSpec((Bc, D), lambda s, n: (0, 0)),
                  pl.BlockSpec((None, D, tn), lambda s, n: (s, 0, n)),
                  pl.BlockSpec((None, 1, tn), lambda s, n: (s, 0, n))],
        out_specs=pl.BlockSpec((None, Bc, tn), lambda s, n: (s, 0, n)),
        out_shape=jax.ShapeDtypeStruct((S, Bc, D3), F32),
        compiler_params=_cparams(("arbitrary", "arbitrary")),
        name="ada",
    )(c_all, ada_w, ada_b.reshape(S, 1, D3))


def _rot_tables(pos):
    posf = pos.astype(F32)[:, None]
    lane = jnp.arange(LANES)

    def table(period, half, rot_dim, active):
        i = lane % period
        in_rot = (i < rot_dim) & active
        inv = 1.0 / (ROPE_THETA ** ((i % half).astype(F32) * 2.0 / rot_dim))
        ang = posf * inv[None, :]
        cos = jnp.where(in_rot[None, :], jnp.cos(ang), 1.0)
        sin = jnp.where(in_rot[None, :], jnp.where((i < half)[None, :], -jnp.sin(ang), jnp.sin(ang)), 0.0)
        return [cos, sin]

    h = ROT_DIM // 2
    t = (table(HEAD_DIM, h, ROT_DIM, lane >= 0)
         + table(HEAD_DIM, h, ROT_DIM, (lane % LANES) < HEAD_DIM)
         + table(LANES, MLA_ROPE // 2, MLA_ROPE, lane >= 0)
         + table(MLA_ROPE, MLA_ROPE // 2, MLA_ROPE, lane >= 0))
    return jnp.stack(t).astype(F32)


def _inproj_kernel(x_ref, sh_ref, sc_ref, rot_ref, w_ref, qn_ref, wuq_ref, wqc_ref, kvn_ref, fbf_ref,
                   qn_o, rows_o, win_o, selwin_o, gate_o, mq_o, lat_o, rope_o, mkey_o,
                   fq_o, frows_o, fkv_o, logf_o, bg_o, fvt_o, latt_o, swt_o):
    D = x_ref.shape[-1]
    h = x_ref[...] * (1.0 + sc_ref[...]) + sh_ref[...]
    hb = h.astype(BF16)
    lane = _iota((1, LANES), 1)

    def proj(a, width):
        return _dot(hb, w_ref[:, a:a + width])

    def rot(xc, t, half, period):
        cos = rot_ref[2 * t]
        sin = rot_ref[2 * t + 1]
        fwd = pltpu.roll(xc, LANES - half, 1)
        bwd = pltpu.roll(xc, half, 1)
        first = (lane & (period - 1)) < half
        return xc * cos + jnp.where(first, fwd, bwd) * sin

    hr = ROT_DIM // 2
    q = proj(SEG_Q, 256)
    for c in range(2):
        qc = rot(q[:, c * LANES:(c + 1) * LANES], 0, hr, HEAD_DIM)
        qn_o[:, c * LANES:(c + 1) * LANES] = (qc * ATT_SCALE).astype(BF16)
    kv = proj(SEG_KV, 384)
    for c in range(3):
        r = rot(kv[:, c * LANES:(c + 1) * LANES], 1, hr, HEAD_DIM)
        if c < 2:
            rows_o[:, c * LANES:(c + 1) * LANES] = r
        else:
            win_o[...] = r
        if c >= 1:
            selwin_o[:, (c - 1) * LANES:c * LANES] = r.astype(BF16)
            swt_o[(c - 1) * LANES:c * LANES, :] = r.T.astype(BF16)
    gate_o[...] = jax.nn.sigmoid(proj(SEG_G, LANES))
    dq = proj(SEG_DQ, MLA_Q_LORA)
    dqn = dq * lax.rsqrt(jnp.mean(dq * dq, axis=-1, keepdims=True) + LN_EPS) * qn_ref[...]
    qh = _dot(dqn.astype(BF16), wuq_ref[...])
    qrope = rot(qh[:, 256:384], 3, MLA_ROPE // 2, MLA_ROPE)
    mq = _dot(qh[:, :256].astype(BF16), wqc_ref[0:256, :]) + _dot(qrope.astype(BF16), wqc_ref[256:384, :])
    mq_o[...] = (mq * MLA_SCALE).astype(BF16)
    lat = proj(SEG_LAT, 256)
    latc = lat[:, :LANES]
    latn = latc * lax.rsqrt(jnp.mean(latc * latc, axis=-1, keepdims=True) + LN_EPS) * kvn_ref[...]
    ropec = rot(lat[:, LANES:], 2, MLA_ROPE // 2, MLA_ROPE)
    lat_o[...] = latn
    rope_o[...] = ropec[:, :MLA_ROPE]
    mkey_o[:, :LANES] = latn.astype(BF16)
    mkey_o[:, LANES:] = ropec.astype(BF16)
    latt_o[...] = latn.T.astype(BF16)
    fqkv = proj(SEG_FQKV, 768)
    fq_o[...] = (fqkv[:, :256] * ATT_SCALE).astype(BF16)
    frows_o[...] = fqkv[:, 256:]
    fkv_o[...] = fqkv[:, 256:].astype(BF16)
    fvt_o[...] = fqkv[:, 512:].T.astype(BF16)
    z = proj(SEG_FF, LANES) + fbf_ref[...]
    logf_o[...] = jnp.minimum(z, 0.0) - jnp.log(1.0 + jnp.exp(-jnp.abs(z)))
    for c in range(N_BRANCH):
        bg_o[:, c * D:(c + 1) * D] = jax.nn.sigmoid(proj(SEG_BG + c * D, D)).astype(BF16)


def _inproj(x, mod, rot, wts, tm):
    G, R, D = x.shape
    Rm = mod.shape[1]
    Rt = rot.shape[1]
    tmm = tm if Rm > 1 else 1
    tmt = tm if Rt > 1 else 1
    w_in, qn, wuq, wqc, kvn, fbf = wts
    widths = [(256, BF16), (256, F32), (128, F32), (256, BF16), (128, F32), (N_HEADS * MLA_QW, BF16),
              (128, F32), (MLA_ROPE, F32), (256, BF16), (256, BF16), (512, F32), (512, BF16), (128, F32),
              (N_BRANCH * D, BF16)]
    const = lambda g, i: (0, 0)
    row = lambda g, i: (g, i, 0)
    return pl.pallas_call(
        _inproj_kernel,
        grid=(G, R // tm),
        in_specs=[pl.BlockSpec((None, tm, D), row),
                  pl.BlockSpec((None, tmm, D), (lambda g, i: (g, i, 0)) if Rm > 1 else (lambda g, i: (g, 0, 0))),
                  pl.BlockSpec((None, tmm, D), (lambda g, i: (g, i, 1)) if Rm > 1 else (lambda g, i: (g, 0, 1))),
                  pl.BlockSpec((8, tmt, LANES), (lambda g, i: (0, i, 0)) if Rt > 1 else (lambda g, i: (0, 0, 0))),
                  pl.BlockSpec(w_in.shape, const),
                  pl.BlockSpec(qn.shape, const),
                  pl.BlockSpec(wuq.shape, const),
                  pl.BlockSpec(wqc.shape, const),
                  pl.BlockSpec(kvn.shape, const),
                  pl.BlockSpec(fbf.shape, const)],
        out_specs=([pl.BlockSpec((None, tm, w), row) for w, _ in widths]
                   + [pl.BlockSpec((None, w, tm), lambda g, i: (g, 0, i)) for w in (256, MLA_KV_LORA, 256)]),
        out_shape=([jax.ShapeDtypeStruct((G, R, w), dt) for w, dt in widths]
                   + [jax.ShapeDtypeStruct((G, w, R), BF16) for w in (256, MLA_KV_LORA, 256)]),
        compiler_params=_cparams(("arbitrary", "arbitrary")),
        name="inproj",
    )(x, mod, mod, rot, w_in, qn, wuq, wqc, kvn, fbf)


def _cumsum_kernel(lf_ref, fcol_ref, frow_ref, carry):
    tc = lf_ref.shape[0]

    @pl.when(pl.program_id(1) == 0)
    def _():
        carry[...] = jnp.zeros_like(carry)

    lower = (_iota((tc, tc), 1) <= _iota((tc, tc), 0)).astype(F32)
    f = _dot(lower, lf_ref[...], HIGHEST) + carry[...]
    fcol_ref[...] = f
    carry[...] = f[tc - 1:tc, :]
    frow_ref[...] = f.T[0:SUBLANES, :]


def _fox_cumsum(logf, tc):
    B, T, _ = logf.shape
    return pl.pallas_call(
        _cumsum_kernel,
        grid=(B, T // tc),
        in_specs=[pl.BlockSpec((None, tc, LANES), lambda b, i: (b, i, 0))],
        out_specs=[pl.BlockSpec((None, tc, LANES), lambda b, i: (b, i, 0)),
                   pl.BlockSpec((None, SUBLANES, tc), lambda b, i: (b, 0, i))],
        out_shape=[jax.ShapeDtypeStruct((B, T, LANES), F32),
                   jax.ShapeDtypeStruct((B, SUBLANES, T), F32)],
        scratch_shapes=[pltpu.VMEM((1, LANES), F32)],
        compiler_params=_cparams(("arbitrary", "arbitrary")),
        name="fox_cumsum",
    )(logf)


def _softmax_step(s, mask, m, l, axis):
    if mask is not None:
        s = jnp.where(mask, s, NEG_INF)
    m_new = jnp.maximum(m, jnp.max(s, axis=axis, keepdims=True))
    alpha = jnp.exp(m - m_new)
    p = jnp.exp(s - m_new)
    if mask is not None:
        p = jnp.where(mask, p, 0.0)
    l = alpha * l + jnp.sum(p, axis=axis, keepdims=True)
    return m_new, l, alpha, p.astype(BF16)


def _attend_heads(scores, mask, carry, axis, values):
    mid = [_softmax_step(s, mask, c[0], c[1], axis) for s, c in zip(scores, carry)]
    return tuple((m, l, alpha * c[2] + values(h, p)) for h, ((m, l, alpha, p), c) in enumerate(zip(mid, carry)))


def _normalize(acc, l):
    return jnp.where(l > 0.0, acc / jnp.where(l > 0.0, l, 1.0), 0.0)


def _fox_kernel(q_ref, k_ref, vt_ref, fcol_ref, frow_ref, o_ref, *, tk):
    tq = q_ref.shape[0]
    i = pl.program_id(1)
    qpos = i * tq + _iota((1, tq), 1)
    assert tq == tk
    qs = [q_ref[:, h * HEAD_DIM:(h + 1) * HEAD_DIM] for h in range(N_HEADS)]
    fqs = [frow_ref[h:h + 1, :] for h in range(N_HEADS)]

    def tile(j, carry, diagonal):
        ks = pl.multiple_of(j * tk, tk)
        mask = ((ks + _iota((tk, 1), 0)) <= qpos) if diagonal else None
        scores = []
        for h in range(N_HEADS):
            k = k_ref[pl.ds(ks, tk), h * HEAD_DIM:(h + 1) * HEAD_DIM]
            scores.append(_nt_dot(k, qs[h]) + (fqs[h] - fcol_ref[pl.ds(ks, tk), h:h + 1]))
        return _attend_heads(scores, mask, carry, 0,
                             lambda h, p: _dot(vt_ref[h * HEAD_DIM:(h + 1) * HEAD_DIM, pl.ds(ks, tk)], p))

    init = tuple((jnp.full((1, tq), NEG_INF, F32), jnp.zeros((1, tq), F32), jnp.zeros((HEAD_DIM, tq), F32))
                 for _ in range(N_HEADS))
    st = lax.fori_loop(0, i, lambda j, c: tile(j, c, False), init)
    st = tile(i, st, True)
    o_t = jnp.concatenate([_normalize(acc, l) for _, l, acc in st], axis=0)
    o_ref[...] = o_t.T.astype(o_ref.dtype)


def _fox_prompt(q, kv, v_t, fcol, frow, tq):
    B, T, _ = q.shape
    return pl.pallas_call(
        functools.partial(_fox_kernel, tk=tq),
        grid=(B, T // tq),
        in_specs=[pl.BlockSpec((None, tq, 256), lambda b, i: (b, i, 0)),
                  pl.BlockSpec((None, T, 256), lambda b, i: (b, 0, 0)),
                  pl.BlockSpec((None, 256, T), lambda b, i: (b, 0, 0)),
                  pl.BlockSpec((None, T, LANES), lambda b, i: (b, 0, 0)),
                  pl.BlockSpec((None, SUBLANES, tq), lambda b, i: (b, 0, i))],
        out_specs=pl.BlockSpec((None, tq, 256), lambda b, i: (b, i, 0)),
        out_shape=jax.ShapeDtypeStruct((B, T, 256), BF16),
        compiler_params=_cparams(("arbitrary", "arbitrary")),
        name="fox_prompt",
    )(q, kv, v_t, fcol, frow)


def _mla_kernel(q_ref, key_ref, latt_ref, o_ref, *, tk):
    tq = q_ref.shape[0]
    i = pl.program_id(1)
    qpos = i * tq + _iota((1, tq), 1)
    assert tq == tk
    qs = [q_ref[:, h * MLA_QW:(h + 1) * MLA_QW] for h in range(N_HEADS)]

    def tile(j, carry, diagonal):
        ks = pl.multiple_of(j * tk, tk)
        mask = ((ks + _iota((tk, 1), 0)) <= qpos) if diagonal else None
        k = key_ref[pl.ds(ks, tk), :]
        scores = [_nt_dot(k, qs[h]) for h in range(N_HEADS)]
        return _attend_heads(scores, mask, carry, 0, lambda h, p: _dot(latt_ref[:, pl.ds(ks, tk)], p))

    init = tuple((jnp.full((1, tq), NEG_INF, F32), jnp.zeros((1, tq), F32), jnp.zeros((MLA_KV_LORA, tq), F32))
                 for _ in range(N_HEADS))
    st = lax.fori_loop(0, i, lambda j, c: tile(j, c, False), init)
    st = tile(i, st, True)
    o_t = jnp.concatenate([_normalize(acc, l) for _, l, acc in st], axis=0)
    o_ref[...] = o_t.T.astype(o_ref.dtype)


def _mla_prompt(q, key, lat_t, tq):
    B, T, _ = q.shape
    return pl.pallas_call(
        functools.partial(_mla_kernel, tk=tq),
        grid=(B, T // tq),
        in_specs=[pl.BlockSpec((None, tq, N_HEADS * MLA_QW), lambda b, i: (b, i, 0)),
                  pl.BlockSpec((None, T, MLA_QW), lambda b, i: (b, 0, 0)),
                  pl.BlockSpec((None, MLA_KV_LORA, T), lambda b, i: (b, 0, 0))],
        out_specs=pl.BlockSpec((None, tq, N_HEADS * MLA_KV_LORA), lambda b, i: (b, i, 0)),
        out_shape=jax.ShapeDtypeStruct((B, T, N_HEADS * MLA_KV_LORA), BF16),
        compiler_params=_cparams(("arbitrary", "arbitrary")),
        name="mla_prompt",
    )(q, key, lat_t)


def _select_blocks(score, blk_f, n_sel):
    nb = score.shape[-1]
    sel = jnp.zeros(score.shape, jnp.bool_)
    picks = []
    for _ in range(n_sel):
        mx = jnp.max(score, axis=-1, keepdims=True)
        idx = jnp.min(jnp.where(score == mx, blk_f, float(nb)), axis=-1, keepdims=True)
        pick = blk_f == idx
        sel = jnp.logical_or(sel, pick)
        score = jnp.where(pick, -jnp.inf, score)
        picks.append(idx)
    return sel, picks


def _block_scores(imp, blk, qpos):
    cur = qpos >> 6
    valid = blk * SEL_BLOCK <= qpos
    forced = (blk == 0) | (blk == cur) | (blk == cur - 1)
    return jnp.where(forced & valid, 1e6, jnp.where(valid, imp, -1e6))


def _nsa_kernel(q_ref, rows_ref, selwin_ref, swt_ref, gate_ref, cmpt_ref, o_ref, ckv_ref, *, tk):
    tq = q_ref.shape[0]
    T = rows_ref.shape[0]
    nbc = T // CMP_BLOCK
    nbs = T // SEL_BLOCK
    i = pl.program_id(1)

    @pl.when(i == 0)
    def _():
        ct = cmpt_ref[...]
        e = jnp.exp(ct - jnp.max(ct, axis=-1, keepdims=True))
        a = e / (jnp.sum(e, axis=-1, keepdims=True) / float(nbc))
        inblk = (_iota((nbc, T), 1) >> 5) == _iota((nbc, T), 0)
        rows = rows_ref[...]
        ck = _dot(jnp.where(inblk, a[0:1, :], 0.0), rows, HIGHEST)
        cv = _dot(jnp.where(inblk, a[1:2, :], 0.0), rows, HIGHEST)
        ckv_ref[...] = jnp.where(_iota((1, LANES), 1) < HEAD_DIM, ck, cv)

    qpos = i * tq + _iota((tq, 1), 0)
    ckv = ckv_ref[...].astype(BF16)
    ck = ckv[:, :HEAD_DIM]
    cv = ckv[:, HEAD_DIM:]
    cblk = _iota((1, nbc), 1)
    cmask = (cblk + 1) * CMP_BLOCK - 1 <= qpos
    psum = jnp.zeros((tq, nbc), F32)
    qs = [q_ref[:, h * HEAD_DIM:(h + 1) * HEAD_DIM] for h in range(N_HEADS)]
    cmp_scores = [_nt_dot(q, ck) for q in qs]
    cmp_probs = []
    for h in range(N_HEADS):
        s = jnp.where(cmask, cmp_scores[h], NEG_INF)
        p = jnp.where(cmask, jnp.exp(s - jnp.max(s, axis=-1, keepdims=True)), 0.0)
        den = jnp.sum(p, axis=-1, keepdims=True)
        p = p * jnp.where(den > 0.0, 1.0 / jnp.where(den > 0.0, den, 1.0), 0.0)
        cmp_probs.append(p.astype(BF16))
        psum = psum + p
    o_c = [_dot(p, cv) for p in cmp_probs]
    pair = ((_iota((nbc, nbs), 0) >> 1) == _iota((nbc, nbs), 1)).astype(F32)
    imp = _dot(psum, pair, HIGHEST)
    blk = _iota((1, nbs), 1)
    sel, _ = _select_blocks(_block_scores(imp, blk, qpos), blk.astype(F32), min(TOP_N, nbs))
    sel_b = sel.astype(BF16)

    qpos_row = i * tq + _iota((1, tq), 1)

    def init():
        return tuple((jnp.full((1, tq), NEG_INF, F32), jnp.zeros((1, tq), F32),
                      jnp.zeros((HEAD_DIM, tq), F32)) for _ in range(N_HEADS))

    def sel_body(j, carry):
        ks = pl.multiple_of(j * tk, tk)
        kpos = ks + _iota((tk, 1), 0)
        expand = (((ks + _iota((tk, nbs), 0)) >> 6) == _iota((tk, nbs), 1)).astype(BF16)
        mask = (_nt_dot(expand, sel_b) > 0.5) & (kpos <= qpos_row)
        k = selwin_ref[pl.ds(ks, tk), 0:HEAD_DIM]
        return _attend_heads([_nt_dot(k, q) for q in qs], mask, carry, 0,
                             lambda h, p: _dot(swt_ref[HEAD_DIM:2 * HEAD_DIM, pl.ds(ks, tk)], p))

    n_kv = (i * tq) // tk + tq // tk
    st_s = lax.fori_loop(0, n_kv, sel_body, init())

    def win_body(j, carry):
        ks = pl.multiple_of(j * tk, tk)
        dist = qpos_row - (ks + _iota((tk, 1), 0))
        mask = (dist >= 0) & (dist <= WINDOW)
        k = selwin_ref[pl.ds(ks, tk), 2 * HEAD_DIM:3 * HEAD_DIM]
        return _attend_heads([_nt_dot(k, q) for q in qs], mask, carry, 0,
                             lambda h, p: _dot(swt_ref[3 * HEAD_DIM:4 * HEAD_DIM, pl.ds(ks, tk)], p))

    first = jnp.maximum((i * tq - WINDOW) // tk, 0)
    st_w = lax.fori_loop(first, n_kv, win_body, init())

    o_s = jnp.concatenate([_normalize(acc, l) for _, l, acc in st_s], axis=0).T
    o_w = jnp.concatenate([_normalize(acc, l) for _, l, acc in st_w], axis=0).T
    g = gate_ref[...]
    outs = []
    for h in range(N_HEADS):
        cols = slice(h * HEAD_DIM, (h + 1) * HEAD_DIM)
        outs.append(g[:, h:h + 1] * o_c[h] + g[:, N_HEADS + h:N_HEADS + h + 1] * o_s[:, cols]
                    + g[:, 2 * N_HEADS + h:2 * N_HEADS + h + 1] * o_w[:, cols])
    o_ref[...] = jnp.concatenate(outs, axis=-1).astype(o_ref.dtype)


def _nsa_prompt(q, rows, selwin, selwin_t, gates, cmp_t, tq):
    B, T, _ = q.shape
    assert T % SEL_BLOCK == 0 and T % tq == 0
    return pl.pallas_call(
        functools.partial(_nsa_kernel, tk=tq),
        grid=(B, T // tq),
        in_specs=[pl.BlockSpec((None, tq, 256), lambda b, i: (b, i, 0)),
                  pl.BlockSpec((None, T, LANES), lambda b, i: (b, 0, 0)),
                  pl.BlockSpec((None, T, 256), lambda b, i: (b, 0, 0)),
                  pl.BlockSpec((None, 256, T), lambda b, i: (b, 0, 0)),
                  pl.BlockSpec((None, tq, LANES), lambda b, i: (b, i, 0)),
                  pl.BlockSpec((2, T), lambda b, i: (0, 0))],
        out_specs=pl.BlockSpec((None, tq, 256), lambda b, i: (b, i, 0)),
        out_shape=jax.ShapeDtypeStruct((B, T, 256), BF16),
        scratch_shapes=[pltpu.VMEM((T // CMP_BLOCK, LANES), F32)],
        compiler_params=_cparams(("arbitrary", "arbitrary")),
        name="nsa_prompt",
    )(q, rows, selwin, selwin_t, gates, cmp_t)


FOX_PAGES_PER_STEP = 32
MLA_PAGES_PER_STEP = 64


def _pages_per_step(n_pages, cap):
    g = min(n_pages, cap)
    while n_pages % g:
        g -= 1
    return g


def _fox_dec_kernel(pt_ref, q_ref, new_ref, lfn_ref, *refs, G):
    kv_refs = refs[:G]
    lf_refs = refs[G:2 * G]
    o_ref = refs[2 * G]
    m_ref, l_ref, acc_ref, car_ref = refs[2 * G + 1:]
    s_id = pl.program_id(1)
    q = q_ref[...]

    @pl.when(s_id == 0)
    def _():
        new = new_ref[...]
        m_ref[...] = jnp.sum(q * new[:, :256], axis=-1, keepdims=True)
        l_ref[...] = jnp.ones_like(l_ref)
        acc_ref[...] = jnp.broadcast_to(new[:, 256:], acc_ref.shape)
        car_ref[...] = lfn_ref[...]

    qb = q.astype(BF16)
    later = (_iota((LANES, LANES), 0) > _iota((LANES, LANES), 1)).astype(F32)
    lf_all = jnp.concatenate([lf_refs[g][...] for g in range(G)], axis=0)
    within = _dot(lf_all, later, HIGHEST)
    total = jnp.sum(lf_all, axis=-1, keepdims=True)
    carry = car_ref[...]
    parts = []
    for g in range(G):
        rows = slice(g * SUBLANES, (g + 1) * SUBLANES)
        parts.append(_dot(qb, kv_refs[g][0].astype(BF16)) + within[rows] + carry)
        carry = carry + total[rows]
    car_ref[...] = carry
    s = jnp.concatenate(parts, axis=-1)
    m_old = m_ref[...]
    m_new = jnp.maximum(m_old, jnp.max(s, axis=-1, keepdims=True))
    alpha = jnp.exp(m_old - m_new)
    p = jnp.exp(s - m_new)
    l_ref[...] = alpha * l_ref[...] + jnp.sum(p, axis=-1, keepdims=True)
    m_ref[...] = m_new
    pb = p.astype(BF16)
    pv = None
    for g in range(G):
        t = _nt_dot(pb[:, g * LANES:(g + 1) * LANES], kv_refs[g][1].astype(BF16))
        pv = t if pv is None else pv + t
    acc_ref[...] = alpha * acc_ref[...] + pv

    @pl.when(s_id == pl.num_programs(1) - 1)
    def _():
        o = acc_ref[...] / l_ref[...]
        own = (_iota(o.shape, 1) >> 6) == _iota(o.shape, 0)
        o_ref[...] = jnp.sum(jnp.where(own, o, 0.0), axis=0, keepdims=True)


def _fox_decode(layer, page_table, q_bd, new_row, lf_new, cache_kv, cache_lf_t):
    DB, NP = page_table.shape
    G = _pages_per_step(NP, FOX_PAGES_PER_STEP)
    pt = page_table.reshape(-1)

    def page(g, ndim):
        return lambda b, s, pt: (layer, pt[b * NP + NP - 1 - (s * G + g)]) + (0,) * ndim

    grid_spec = pltpu.PrefetchScalarGridSpec(
        num_scalar_prefetch=1,
        grid=(DB, NP // G),
        in_specs=([pl.BlockSpec((None, SUBLANES, 256), lambda b, s, pt: (b, 0, 0)),
                   pl.BlockSpec((None, 1, 512), lambda b, s, pt: (b, 0, 0)),
                   pl.BlockSpec((None, SUBLANES, 1), lambda b, s, pt: (b, 0, 0))]
                  + [pl.BlockSpec((None, None, 2, 256, LANES), page(g, 3)) for g in range(G)]
                  + [pl.BlockSpec((None, None, SUBLANES, LANES), page(g, 2)) for g in range(G)]),
        out_specs=pl.BlockSpec((None, 1, 256), lambda b, s, pt: (b, 0, 0)),
        scratch_shapes=[pltpu.VMEM((SUBLANES, 1), F32), pltpu.VMEM((SUBLANES, 1), F32),
                        pltpu.VMEM((SUBLANES, 256), F32), pltpu.VMEM((SUBLANES, 1), F32)])
    return pl.pallas_call(
        functools.partial(_fox_dec_kernel, G=G),
        grid_spec=grid_spec,
        out_shape=jax.ShapeDtypeStruct((DB, 1, 256), F32),
        compiler_params=_cparams(("arbitrary", "arbitrary")),
        name="fox_decode",
    )(pt, q_bd, new_row, lf_new, *([cache_kv] * G), *([cache_lf_t] * G))


def _mla_dec_kernel(pt_ref, q_ref, new_ref, *refs, G):
    lat_refs = refs[:G]
    rope_refs = refs[G:2 * G]
    o_ref = refs[2 * G]
    m_ref, l_ref, acc_ref = refs[2 * G + 1:]
    s_id = pl.program_id(1)
    q = q_ref[...]

    @pl.when(s_id == 0)
    def _():
        new = new_ref[...]
        m_ref[...] = jnp.sum(q * new, axis=-1, keepdims=True)
        l_ref[...] = jnp.ones_like(l_ref)
        acc_ref[...] = jnp.broadcast_to(new[:, :MLA_KV_LORA], acc_ref.shape)

    q_lat = q[:, :MLA_KV_LORA].astype(BF16)
    q_rope = q[:, MLA_KV_LORA:].astype(BF16)
    rope_pad = jnp.zeros((LANES - MLA_ROPE, LANES), F32)
    parts = []
    for g in range(G):
        rope_t = jnp.concatenate([rope_refs[g][...], rope_pad], axis=0).astype(BF16)
        parts.append(_nt_dot(q_lat, lat_refs[g][...].astype(BF16)) + _dot(q_rope, rope_t))
    s = jnp.concatenate(parts, axis=-1)
    m_old = m_ref[...]
    m_new = jnp.maximum(m_old, jnp.max(s, axis=-1, keepdims=True))
    alpha = jnp.exp(m_old - m_new)
    p = jnp.exp(s - m_new)
    l_ref[...] = alpha * l_ref[...] + jnp.sum(p, axis=-1, keepdims=True)
    m_ref[...] = m_new
    pb = p.astype(BF16)
    pv = None
    for g in range(G):
        t = _dot(pb[:, g * LANES:(g + 1) * LANES], lat_refs[g][...].astype(BF16))
        pv = t if pv is None else pv + t
    acc_ref[...] = alpha * acc_ref[...] + pv

    @pl.when(s_id == pl.num_programs(1) - 1)
    def _():
        o_ref[...] = acc_ref[...] / l_ref[...]


def _mla_decode(layer, page_table, q_h, new_key, cache_lat, cache_rope):
    DB, NP = page_table.shape
    G = _pages_per_step(NP, MLA_PAGES_PER_STEP)
    pt = page_table.reshape(-1)

    def page(g):
        return lambda b, s, pt: (layer, pt[b * NP + s * G + g], 0, 0)

    grid_spec = pltpu.PrefetchScalarGridSpec(
        num_scalar_prefetch=1,
        grid=(DB, NP // G),
        in_specs=([pl.BlockSpec((None, SUBLANES, MLA_QW), lambda b, s, pt: (b, 0, 0)),
                   pl.BlockSpec((None, 1, MLA_QW), lambda b, s, pt: (b, 0, 0))]
                  + [pl.BlockSpec((None, None, LANES, MLA_KV_LORA), page(g)) for g in range(G)]
                  + [pl.BlockSpec((None, None, MLA_ROPE, LANES), page(g)) for g in range(G)]),
        out_specs=pl.BlockSpec((None, SUBLANES, MLA_KV_LORA), lambda b, s, pt: (b, 0, 0)),
        scratch_shapes=[pltpu.VMEM((SUBLANES, 1), F32), pltpu.VMEM((SUBLANES, 1), F32),
                        pltpu.VMEM((SUBLANES, MLA_KV_LORA), F32)])
    return pl.pallas_call(
        functools.partial(_mla_dec_kernel, G=G),
        grid_spec=grid_spec,
        out_shape=jax.ShapeDtypeStruct((DB, SUBLANES, MLA_KV_LORA), F32),
        compiler_params=_cparams(("arbitrary", "arbitrary")),
        name="mla_decode",
    )(pt, q_h, new_key, *([cache_lat] * G), *([cache_rope] * G))


def _nsa_cmp_kernel(pt_ref, q_ref, cmpt_ref, *refs, G, past_len):
    page_refs = refs[:G]
    sel_ref, oc_ref = refs[G:G + 2]
    ckv_ref = refs[G + 2]
    per_page = LANES // CMP_BLOCK
    ct = cmpt_ref[...]
    e = jnp.exp(ct - jnp.max(ct, axis=-1, keepdims=True))
    a = e / (jnp.sum(e, axis=-1, keepdims=True) / float(per_page))
    wrow = _iota((4 * per_page, LANES), 0)
    is_v_row = ((wrow >> 2) & 1) == 1
    w = jnp.where((_iota((4 * per_page, LANES), 1) >> 5) == (wrow & 3), jnp.where(is_v_row, a[1:2, :], a[0:1, :]), 0.0)
    w_hi = w.astype(BF16).astype(F32)
    wb = jnp.where(wrow < 2 * per_page, w_hi, w - w_hi).astype(BF16)
    is_k = _iota((1, LANES), 1) < HEAD_DIM
    for g in range(G):
        page = page_refs[g][...].reshape(2 * HEAD_DIM, LANES).astype(BF16)
        r = _nt_dot(wb, page)
        t = r[0:2 * per_page] + r[2 * per_page:]
        ckv_ref[g * per_page:(g + 1) * per_page, :] = jnp.where(is_k, t[0:per_page], t[per_page:])

    nbc = ckv_ref.shape[0]
    nb_past = nbc // (SEL_BLOCK // CMP_BLOCK)
    width = sel_ref.shape[-1]
    qpos = past_len
    ckv = ckv_ref[...].astype(BF16)
    q = q_ref[...].astype(BF16)
    cmask = (_iota((1, nbc), 1) + 1) * CMP_BLOCK - 1 <= qpos
    s = jnp.where(cmask, _nt_dot(q, ckv[:, :HEAD_DIM]), NEG_INF)
    p = jnp.where(cmask, jnp.exp(s - jnp.max(s, axis=-1, keepdims=True)), 0.0)
    den = jnp.sum(p, axis=-1, keepdims=True)
    p = p * jnp.where(den > 0.0, 1.0 / jnp.where(den > 0.0, den, 1.0), 0.0)
    oc_ref[...] = _dot(p.astype(BF16), ckv[:, HEAD_DIM:])
    head = _iota((SUBLANES, 1), 0) < N_HEADS
    psum = jnp.sum(jnp.where(head, p, 0.0), axis=0, keepdims=True)
    pair = ((_iota((nbc, width), 0) >> 1) == _iota((nbc, width), 1)).astype(F32)
    imp = _dot(jnp.broadcast_to(psum, (SUBLANES, nbc)), pair, HIGHEST)[0:1, :]
    blk = _iota((1, width), 1)
    n_sel = min(TOP_N, nb_past + 1)
    _, picks = _select_blocks(_block_scores(imp, blk, qpos), blk.astype(F32), n_sel)
    out = jnp.full((1, width), -1.0, F32)
    for r, idx in enumerate(picks):
        out = jnp.where(blk == r, idx, out)
    sel_ref[...] = out.astype(I32)


def _nsa_cmp_decode(layer, page_table, q_h, cmp_page, cache_t, past_len):
    DB, NP = page_table.shape
    pt = page_table.reshape(-1)
    nbc = NP * (LANES // CMP_BLOCK)
    nb_past = nbc // 2
    width = (nb_past + 1 + LANES - 1) // LANES * LANES

    def page(g):
        return lambda b, pt: (layer, pt[b * NP + g], 0, 0, 0)

    grid_spec = pltpu.PrefetchScalarGridSpec(
        num_scalar_prefetch=1,
        grid=(DB,),
        in_specs=([pl.BlockSpec((None, SUBLANES, HEAD_DIM), lambda b, pt: (b, 0, 0)),
                   pl.BlockSpec((2, LANES), lambda b, pt: (0, 0))]
                  + [pl.BlockSpec((None, None, 2, HEAD_DIM, LANES), page(g)) for g in range(NP)]),
        out_specs=[pl.BlockSpec((None, 1, width), lambda b, pt: (b, 0, 0)),
                   pl.BlockSpec((None, SUBLANES, HEAD_DIM), lambda b, pt: (b, 0, 0))],
        scratch_shapes=[pltpu.VMEM((nbc, LANES), F32)])
    return pl.pallas_call(
        functools.partial(_nsa_cmp_kernel, G=NP, past_len=past_len),
        grid_spec=grid_spec,
        out_shape=[jax.ShapeDtypeStruct((DB, 1, width), I32),
                   jax.ShapeDtypeStruct((DB, SUBLANES, HEAD_DIM), F32)],
        compiler_params=_cparams(("arbitrary",)),
        name="nsa_cmp_decode",
    )(pt, q_h, cmp_page, *([cache_t] * NP))


def _nsa_sel_kernel(pt_ref, sel_ref, q_ref, oc_ref, new_ref, gate_ref, *refs, n_sel, nb_past):
    blk_refs = refs[:n_sel]
    win_ref, o_ref = refs[n_sel:]
    b = pl.program_id(0)
    q = q_ref[...]
    qb = q.astype(BF16)
    new = new_ref[...]
    half_of_lane = _iota((1, LANES), 1) >> 6
    parts, masks = [], []
    n_new = jnp.int32(0)
    for r in range(n_sel):
        blk = sel_ref[b * n_sel + r]
        half = jnp.where((blk >= 0) & (blk < nb_past), blk & 1, -1)
        masks.append(half_of_lane == jnp.full((1, LANES), half, I32))
        parts.append(_dot(qb, blk_refs[r][0].astype(BF16)))
        n_new = n_new + jnp.where(blk == nb_past, 1, 0)
    has_new = jnp.full((1, 1), n_new, I32) > 0

    def attend(s, mask, k_new, v_new, new_ok, values):
        if mask is not None:
            s = jnp.where(mask, s, NEG_INF)
        s_new = jnp.where(new_ok, jnp.sum(q * k_new, axis=-1, keepdims=True), NEG_INF)
        m = jnp.maximum(jnp.max(s, axis=-1, keepdims=True), s_new)
        p = jnp.exp(s - m)
        if mask is not None:
            p = jnp.where(mask, p, 0.0)
        p_new = jnp.where(new_ok, jnp.exp(s_new - m), 0.0)
        l = jnp.sum(p, axis=-1, keepdims=True) + p_new
        return _normalize(values(p.astype(BF16)) + p_new * v_new, l)

    def sel_values(pb):
        pv = None
        for r in range(n_sel):
            t = _nt_dot(pb[:, r * LANES:(r + 1) * LANES], blk_refs[r][1].astype(BF16))
            pv = t if pv is None else pv + t
        return pv

    o_s = attend(jnp.concatenate(parts, axis=-1), jnp.concatenate(masks, axis=-1),
                 new[:, 0:HEAD_DIM], new[:, HEAD_DIM:2 * HEAD_DIM], has_new, sel_values)
    o_w = attend(_dot(qb, win_ref[0].astype(BF16)), None,
                 new[:, 2 * HEAD_DIM:3 * HEAD_DIM], new[:, 3 * HEAD_DIM:], jnp.ones((1, 1), jnp.bool_),
                 lambda pb: _nt_dot(pb, win_ref[1].astype(BF16)))
    g = gate_ref[...]
    o_ref[...] = g[:, 0:1] * oc_ref[...] + g[:, 1:2] * o_s + g[:, 2:3] * o_w


def _nsa_sel_decode(layer, page_table, sel, q_h, o_c, new_row, gates_t, cache_t, win_t):
    DB, NP = page_table.shape
    nb_past = NP * (LANES // SEL_BLOCK)
    n_sel = min(TOP_N, nb_past + 1)
    WB = win_t.shape[-1]
    pt = page_table.reshape(-1)
    sel_flat = sel[:, 0, :n_sel].reshape(-1)

    def block(r):
        def index(b, pt, sel):
            blk = jnp.clip(sel[b * n_sel + r], 0, nb_past - 1)
            return (layer, pt[b * NP + blk // 2], 1, 0, 0)
        return index

    row = lambda b, pt, sel: (b, 0, 0)
    grid_spec = pltpu.PrefetchScalarGridSpec(
        num_scalar_prefetch=2,
        grid=(DB,),
        in_specs=([pl.BlockSpec((None, SUBLANES, HEAD_DIM), row),
                   pl.BlockSpec((None, SUBLANES, HEAD_DIM), row),
                   pl.BlockSpec((None, 1, 256), row),
                   pl.BlockSpec((None, SUBLANES, LANES), row)]
                  + [pl.BlockSpec((None, None, 2, HEAD_DIM, LANES), block(r)) for r in range(n_sel)]
                  + [pl.BlockSpec((None, None, 2, HEAD_DIM, WB), lambda b, pt, sel: (layer, b, 0, 0, 0))]),
        out_specs=pl.BlockSpec((None, SUBLANES, HEAD_DIM), row))
    return pl.pallas_call(
        functools.partial(_nsa_sel_kernel, n_sel=n_sel, nb_past=nb_past),
        grid_spec=grid_spec,
        out_shape=jax.ShapeDtypeStruct((DB, SUBLANES, HEAD_DIM), F32),
        compiler_params=_cparams(("arbitrary",)),
        name="nsa_sel_decode",
    )(pt, sel_flat, q_h, o_c, new_row, gates_t, *([cache_t] * n_sel), win_t)


def _layernorm(y, g, b):
    mu = jnp.mean(y, axis=-1, keepdims=True)
    d = y - mu
    var = jnp.mean(d * d, axis=-1, keepdims=True)
    return d * lax.rsqrt(var + LN_EPS) * g + b


def _merge_kernel(x_ref, onsa_ref, olat_ref, ofox_ref, bg_ref, g1_ref, sh2_ref, sc2_ref,
                  wuv_ref, wb_ref, wo_ref, lng_ref, lnb_ref, wr_ref, br_ref,
                  x1_o, h2_o, ri_o, rw_o, *, alpha):
    D = x_ref.shape[-1]
    o_mla = _dot(olat_ref[...], wuv_ref[...]).astype(BF16)
    branches = (onsa_ref[...], o_mla, ofox_ref[...])
    mix = None
    for n in range(N_BRANCH):
        t = bg_ref[:, n * D:(n + 1) * D].astype(F32) * _dot(branches[n], wb_ref[n])
        mix = t if mix is None else mix + t
    mix = _dot(mix.astype(BF16), wo_ref[...])
    x1 = _layernorm(alpha * x_ref[...] + g1_ref[...] * mix, lng_ref[...], lnb_ref[...])
    x1_o[...] = x1
    h2 = x1 * (1.0 + sc2_ref[...]) + sh2_ref[...]
    h2_o[...] = h2.astype(BF16)
    logits = _dot(h2, wr_ref[...], HIGHEST) + br_ref[...]
    lane = _iota((1, LANES), 1)
    lane_f = lane.astype(F32)
    is_g = lane < N_GROUPS
    gmax = jnp.max(jnp.where(is_g, logits, NEG_INF), axis=-1, keepdims=True)
    gidx = jnp.min(jnp.where(is_g & (logits == gmax), lane_f, float(LANES)), axis=-1, keepdims=True)
    g_w = 1.0 / jnp.sum(jnp.where(is_g, jnp.exp(logits - gmax), 0.0), axis=-1, keepdims=True)
    grp_of_lane = ((lane - N_GROUPS) >> 3).astype(F32)
    in_grp = (lane >= N_GROUPS) & (lane < N_GROUPS + N_EXPERTS) & (grp_of_lane == gidx)
    v1 = jnp.max(jnp.where(in_grp, logits, NEG_INF), axis=-1, keepdims=True)
    i1 = jnp.min(jnp.where(in_grp & (logits == v1), lane_f, float(LANES)), axis=-1, keepdims=True)
    rest = in_grp & (lane_f != i1)
    v2 = jnp.max(jnp.where(rest, logits, NEG_INF), axis=-1, keepdims=True)
    i2 = jnp.min(jnp.where(rest & (logits == v2), lane_f, float(LANES)), axis=-1, keepdims=True)
    e21 = jnp.exp(v2 - v1)
    w1 = g_w / (1.0 + e21)
    w2 = g_w * e21 / (1.0 + e21)
    ri_o[...] = jnp.where(lane == 0, i1 - N_GROUPS, jnp.where(lane == 1, i2 - N_GROUPS, 0.0)).astype(I32)
    rw_o[...] = jnp.where(lane == 0, w1, jnp.where(lane == 1, w2, 0.0))


def _merge(x, o_nsa, o_lat, o_fox, bg, mod0, mod1, wts, alpha, tm):
    G, R, D = x.shape
    Rm = mod0.shape[1]
    tmm = tm if Rm > 1 else 1
    wuv, wb, wo, lng, lnb, wr, br = wts
    row = lambda g, i: (g, i, 0)

    def modspec(j):
        return pl.BlockSpec((None, tmm, D), (lambda g, i: (g, i, j)) if Rm > 1 else (lambda g, i: (g, 0, j)))

    def const(a):
        return pl.BlockSpec(a.shape, lambda g, i: (0,) * a.ndim)

    return pl.pallas_call(
        functools.partial(_merge_kernel, alpha=alpha),
        grid=(G, R // tm),
        in_specs=[pl.BlockSpec((None, tm, D), row),
                  pl.BlockSpec((None, tm, 256), row),
                  pl.BlockSpec((None, tm, N_HEADS * MLA_KV_LORA), row),
                  pl.BlockSpec((None, tm, 256), row),
                  pl.BlockSpec((None, tm, N_BRANCH * D), row),
                  modspec(2), ] + [pl.BlockSpec((None, tmm, D), (lambda g, i: (g, i, 0)) if Rm > 1 else (lambda g, i: (g, 0, 0))),
                                   pl.BlockSpec((None, tmm, D), (lambda g, i: (g, i, 1)) if Rm > 1 else (lambda g, i: (g, 0, 1)))]
                 + [const(a) for a in (wuv, wb, wo, lng, lnb, wr, br)],
        out_specs=[pl.BlockSpec((None, tm, D), row), pl.BlockSpec((None, tm, D), row),
                   pl.BlockSpec((None, tm, LANES), row), pl.BlockSpec((None, tm, LANES), row)],
        out_shape=[jax.ShapeDtypeStruct((G, R, D), F32), jax.ShapeDtypeStruct((G, R, D), BF16),
                   jax.ShapeDtypeStruct((G, R, LANES), I32), jax.ShapeDtypeStruct((G, R, LANES), F32)],
        compiler_params=_cparams(("arbitrary", "arbitrary")),
        name="merge",
    )(x, o_nsa, o_lat, o_fox, bg, mod0, mod1, mod1, wuv, wb, wo, lng, lnb, wr, br)


def _ffn_kernel(te_ref, nt_ref, x_ref, wg_ref, wu_ref, wd_ref, o_ref):
    i = pl.program_id(0)

    @pl.when(i < nt_ref[0])
    def _():
        x = x_ref[...]
        gte = _dot(x, wg_ref[...])
        a = gte * jax.nn.sigmoid(gte) * _dot(x, wu_ref[...])
        o_ref[...] = _dot(a.astype(BF16), wd_ref[...]).astype(o_ref.dtype)

    @pl.when(i >= nt_ref[0])
    def _():
        o_ref[...] = jnp.zeros_like(o_ref)


def _expert_ffn(x_sorted, tile_expert, n_tiles_used, wg, wu, wd):
    NP, D = x_sorted.shape
    E, _, FF = wg.shape
    grid_spec = pltpu.PrefetchScalarGridSpec(
        num_scalar_prefetch=2,
        grid=(NP // MOE_TILE,),
        in_specs=[pl.BlockSpec((MOE_TILE, D), lambda i, te, nt: (i, 0)),
                  pl.BlockSpec((None, D, FF), lambda i, te, nt: (te[i], 0, 0)),
                  pl.BlockSpec((None, D, FF), lambda i, te, nt: (te[i], 0, 0)),
                  pl.BlockSpec((None, FF, D), lambda i, te, nt: (te[i], 0, 0))],
        out_specs=pl.BlockSpec((MOE_TILE, D), lambda i, te, nt: (i, 0)))
    return pl.pallas_call(
        _ffn_kernel,
        grid_spec=grid_spec,
        out_shape=jax.ShapeDtypeStruct((NP, D), BF16),
        compiler_params=_cparams(("arbitrary",)),
        name="expert_ffn",
    )(tile_expert, n_tiles_used, x_sorted, wg, wu, wd)


def _dispatch(e_idx, n_tok):
    n_asg = 2 * n_tok
    e_all = e_idx.reshape(-1)
    order = jnp.argsort(e_all, stable=True).astype(I32)
    counts = jnp.sum(e_all[:, None] == jnp.arange(N_EXPERTS, dtype=I32)[None, :], axis=0, dtype=I32)
    starts = jnp.cumsum(counts) - counts
    pcounts = (counts + MOE_TILE - 1) // MOE_TILE * MOE_TILE
    pends = jnp.cumsum(pcounts)
    pstarts = pends - pcounts
    n_tiles = (n_asg + MOE_TILE - 1) // MOE_TILE + N_EXPERTS
    tile_start = jnp.arange(n_tiles, dtype=I32) * MOE_TILE
    tile_expert = jnp.minimum(jnp.sum(tile_start[:, None] >= pends[None, :], axis=1, dtype=I32), N_EXPERTS - 1)
    rows = jnp.arange(n_tiles * MOE_TILE, dtype=I32)
    re = tile_expert[rows // MOE_TILE]
    local = rows - pstarts[re]
    valid = local < counts[re]
    src = jnp.where(valid, order[jnp.clip(starts[re] + local, 0, n_asg - 1)], 0)
    row_token = src % n_tok
    rank = jnp.zeros((n_asg,), I32).at[order].set(jnp.arange(n_asg, dtype=I32))
    dest = pstarts[e_all] + rank - starts[e_all]
    n_used = (pends[-1] // MOE_TILE).astype(I32).reshape(1)
    return row_token, tile_expert, n_used, dest.reshape(2, n_tok)


def _combine_kernel(x_ref, y0_ref, y1_ref, rw_ref, g2_ref, lng_ref, lnb_ref, o_ref, *, alpha):
    rw = rw_ref[...]
    y = rw[:, 0:1] * y0_ref[...].astype(F32) + rw[:, 1:2] * y1_ref[...].astype(F32)
    o_ref[...] = _layernorm(alpha * x_ref[...] + g2_ref[...] * y, lng_ref[...], lnb_ref[...])


def _combine(x1, y0, y1, rw, mod1, lng, lnb, alpha, tm):
    G, R, D = x1.shape
    Rm = mod1.shape[1]
    tmm = tm if Rm > 1 else 1
    row = lambda g, i: (g, i, 0)
    const = lambda g, i: (0, 0)
    return pl.pallas_call(
        functools.partial(_combine_kernel, alpha=alpha),
        grid=(G, R // tm),
        in_specs=[pl.BlockSpec((None, tm, D), row), pl.BlockSpec((None, tm, D), row),
                  pl.BlockSpec((None, tm, D), row), pl.BlockSpec((None, tm, LANES), row),
                  pl.BlockSpec((None, tmm, D), (lambda g, i: (g, i, 2)) if Rm > 1 else (lambda g, i: (g, 0, 2))),
                  pl.BlockSpec((1, D), const), pl.BlockSpec((1, D), const)],
        out_specs=pl.BlockSpec((None, tm, D), row),
        out_shape=jax.ShapeDtypeStruct((G, R, D), F32),
        compiler_params=_cparams(("arbitrary", "arbitrary")),
        name="combine",
    )(x1, y0, y1, rw, mod1, lng, lnb)


def _pad_cols(w, width):
    return jnp.pad(w, ((0, 0), (0, width - w.shape[1])))


def _prep_layer(l, w_in, mla_q_norm, mla_w_uq, mla_kv_norm, mla_w_uk, mla_w_uv, fox_bf, w_branch, w_o,
                ln1_g, ln1_b, moe_w_grp, moe_b_grp, moe_w_exp, moe_b_exp, ln2_g, ln2_b):
    w = w_in[l]
    D = w.shape[0]
    o = [0, 256, 640, 652, 908, 1036, 1068, 1836, 1840, 1840 + N_BRANCH * D]
    w_p = jnp.concatenate([w[:, o[0]:o[2]], _pad_cols(w[:, o[2]:o[3]], LANES), w[:, o[3]:o[5]],
                           _pad_cols(w[:, o[5]:o[6]], LANES), w[:, o[6]:o[7]],
                           _pad_cols(w[:, o[7]:o[8]], LANES), w[:, o[8]:o[9]]], axis=1).astype(BF16)
    uq = mla_w_uq[l].reshape(MLA_Q_LORA, N_HEADS, MLA_NOPE + MLA_ROPE)
    wuq = jnp.concatenate([uq[:, :, :MLA_NOPE].reshape(MLA_Q_LORA, -1),
                           uq[:, :, MLA_NOPE:].reshape(MLA_Q_LORA, -1)], axis=1).astype(BF16)
    uk = mla_w_uk[l]
    wqc = jnp.zeros((N_HEADS * MLA_NOPE + N_HEADS * MLA_ROPE, N_HEADS * MLA_QW), F32)
    eye = jnp.eye(MLA_ROPE, dtype=F32)
    for h in range(N_HEADS):
        wqc = wqc.at[h * MLA_NOPE:(h + 1) * MLA_NOPE, h * MLA_QW:h * MLA_QW + MLA_KV_LORA].set(uk[:, h, :].T)
        r0 = N_HEADS * MLA_NOPE + h * MLA_ROPE
        c0 = h * MLA_QW + MLA_KV_LORA
        wqc = wqc.at[r0:r0 + MLA_ROPE, c0:c0 + MLA_ROPE].set(eye)
    uv = mla_w_uv[l]
    wuv = jnp.zeros((N_HEADS * MLA_KV_LORA, N_HEADS * MLA_V), F32)
    for h in range(N_HEADS):
        wuv = wuv.at[h * MLA_KV_LORA:(h + 1) * MLA_KV_LORA, h * MLA_V:(h + 1) * MLA_V].set(uv[:, h, :])
    fbf = _pad_cols(fox_bf[l][None, :], LANES)
    wr = _pad_cols(jnp.concatenate([moe_w_grp[l], moe_w_exp[l]], axis=1), LANES)
    br = _pad_cols(jnp.concatenate([moe_b_grp[l], moe_b_exp[l]])[None, :], LANES)
    inproj_w = (w_p, mla_q_norm[l][None, :], wuq, wqc.astype(BF16), mla_kv_norm[l][None, :], fbf)
    merge_w = (wuv.astype(BF16), w_branch[l].astype(BF16), w_o[l].astype(BF16),
               ln1_g[l][None, :], ln1_b[l][None, :], wr, br)
    return inproj_w, merge_w, (ln2_g[l][None, :], ln2_b[l][None, :])


def kernel(x_prompt, x_sample, cache_nsa_kv, state_nsa_win, cache_mla_latent, cache_mla_rope, cache_fox_kv, cache_fox_logf, page_table, c_prompt, c_sample, ada_w, ada_b, w_in, nsa_cmp_pos, mla_q_norm, mla_w_uq, mla_kv_norm, mla_w_uk, mla_w_uv, fox_bf, w_branch, w_o, ln1_g, ln1_b, moe_w_grp, moe_b_grp, moe_w_exp, moe_b_exp, moe_w_gate, moe_w_up, moe_w_down, ln2_g, ln2_b):
    B, T, D = x_prompt.shape
    DB = x_sample.shape[0]
    assert x_sample.shape[1] == 1
    depth = ada_w.shape[0]
    n_pool = cache_nsa_kv.shape[1]
    NP = page_table.shape[1]
    past_len = NP * LANES
    assert cache_nsa_kv.shape[2] == LANES
    alpha = (2 * depth) ** 0.25
    tm = min(ROW_TILE, T)
    assert T % tm == 0
    n_keep = min(WINDOW, T)

    c_all = jnp.concatenate([c_prompt, c_sample], axis=0)
    ada = _ada(c_all, ada_w.reshape(depth * 2, D, 3 * D), ada_b.reshape(depth * 2, 3 * D))

    rot_p = _rot_tables(jnp.arange(T, dtype=I32))
    rot_s = _rot_tables(jnp.full((1,), past_len, I32))
    nsa_t = jnp.transpose(cache_nsa_kv, (0, 1, 3, 4, 2))
    win_t = jnp.transpose(state_nsa_win, (0, 1, 3, 4, 2))
    fox_t = jnp.transpose(cache_fox_kv, (0, 1, 3, 4, 5, 2)).reshape(depth, n_pool, 2, 256, LANES)
    rope_t = jnp.transpose(cache_mla_rope, (0, 1, 3, 2))
    lf_t = jnp.pad(jnp.swapaxes(cache_fox_logf, 2, 3), ((0, 0), (0, 0), (0, SUBLANES - N_HEADS), (0, 0)))

    y_p = x_prompt
    y_s = x_sample.reshape(1, DB, D)
    outs_p = [[] for _ in range(6)]
    outs_s = [[] for _ in range(6)]
    head_pad = ((0, 0), (0, SUBLANES - N_HEADS), (0, 0))
    for l in range(depth):
        inproj_w, merge_w, (ln2g, ln2b) = _prep_layer(
            l, w_in, mla_q_norm, mla_w_uq, mla_kv_norm, mla_w_uk, mla_w_uv, fox_bf, w_branch, w_o,
            ln1_g, ln1_b, moe_w_grp, moe_b_grp, moe_w_exp, moe_b_exp, ln2_g, ln2_b)
        mod0_p = ada[2 * l, :B].reshape(B, 1, 3 * D)
        mod1_p = ada[2 * l + 1, :B].reshape(B, 1, 3 * D)
        mod0_s = ada[2 * l, B:].reshape(1, DB, 3 * D)
        mod1_s = ada[2 * l + 1, B:].reshape(1, DB, 3 * D)
        cmp_t = jnp.tile(nsa_cmp_pos[l], (1, T // CMP_BLOCK))
        cmp_page = jnp.tile(nsa_cmp_pos[l], (1, LANES // CMP_BLOCK))

        (qn, rows, win, selwin, gates, mq, lat, rope, mkey, fq, frows, fkv, logf, bg,
         fv_t, lat_t, selwin_t) = _inproj(y_p, mod0_p, rot_p, inproj_w, tm)
        fcol, frow = _fox_cumsum(logf, tm)
        o_nsa = _nsa_prompt(qn, rows, selwin, selwin_t, gates, cmp_t, tm)
        o_lat = _mla_prompt(mq, mkey, lat_t, tm)
        o_fox = _fox_prompt(fq, fkv, fv_t, fcol, frow, tm)
        x1_p, h2_p, ri_p, rw_p = _merge(y_p, o_nsa, o_lat, o_fox, bg, mod0_p, mod1_p, merge_w, alpha, tm)
        outs_p[0].append(rows.reshape(B, T, 4, HEAD_DIM))
        outs_p[1].append(win[:, T - n_keep:].reshape(B, n_keep, 2, HEAD_DIM))
        outs_p[2].append(lat)
        outs_p[3].append(rope)
        outs_p[4].append(frows.reshape(B, T, 2, N_HEADS, HEAD_DIM))
        outs_p[5].append(logf[:, :, :N_HEADS])

        (qn, rows, win, selwin, gates, mq, lat, rope, mkey, fq, frows, fkv, logf, bg, _, _, _) = _inproj(
            y_s, mod0_s, rot_s, inproj_w, DB)
        qn_h = jnp.pad(qn[0].astype(F32).reshape(DB, N_HEADS, HEAD_DIM), head_pad)
        sel, o_c = _nsa_cmp_decode(l, page_table, qn_h, cmp_page, nsa_t, past_len)
        new_sw = jnp.concatenate([rows[0][:, LANES:], win[0]], axis=-1).reshape(DB, 1, 256)
        gates_t = jnp.pad(jnp.swapaxes(gates[0][:, :N_BRANCH * N_HEADS].reshape(DB, N_BRANCH, N_HEADS), 1, 2),
                          ((0, 0), (0, SUBLANES - N_HEADS), (0, LANES - N_BRANCH)))
        o_nsa_s = _nsa_sel_decode(l, page_table, sel, qn_h, o_c, new_sw, gates_t, nsa_t, win_t)
        mq_h = jnp.pad(mq[0].astype(F32).reshape(DB, N_HEADS, MLA_QW), head_pad)
        o_lat_s = _mla_decode(l, page_table, mq_h, mkey[0].astype(F32).reshape(DB, 1, MLA_QW),
                              cache_mla_latent, rope_t)
        fq_h = fq[0].astype(F32).reshape(DB, N_HEADS, 1, HEAD_DIM)
        fq_bd = (fq_h * jnp.eye(N_HEADS, dtype=F32)[None, :, :, None]).reshape(DB, N_HEADS, 256)
        lf_new = jnp.pad(logf[0][:, :N_HEADS, None], head_pad)
        o_fox_s = _fox_decode(l, page_table, jnp.pad(fq_bd, head_pad), frows[0].reshape(DB, 1, 512), lf_new,
                              fox_t, lf_t)
        x1_s, h2_s, ri_s, rw_s = _merge(
            y_s, o_nsa_s[:, :N_HEADS].reshape(1, DB, 256).astype(BF16),
            o_lat_s[:, :N_HEADS].reshape(1, DB, N_HEADS * MLA_KV_LORA).astype(BF16),
            o_fox_s.reshape(1, DB, 256).astype(BF16), bg, mod0_s, mod1_s, merge_w, alpha, DB)
        outs_s[0].append(rows[0].reshape(DB, 1, 4, HEAD_DIM))
        outs_s[1].append(jnp.concatenate([state_nsa_win[l][:, 1:], win[0].reshape(DB, 1, 2, HEAD_DIM)], axis=1))
        outs_s[2].append(lat[0].reshape(DB, 1, MLA_KV_LORA))
        outs_s[3].append(rope[0].reshape(DB, 1, MLA_ROPE))
        outs_s[4].append(frows[0].reshape(DB, 1, 2, N_HEADS, HEAD_DIM))
        outs_s[5].append(logf[0][:, :N_HEADS].reshape(DB, 1, N_HEADS))

        n_tok = B * T + DB
        h2_all = jnp.concatenate([h2_p.reshape(B * T, D), h2_s.reshape(DB, D)], axis=0)
        e_idx = jnp.concatenate([ri_p.reshape(B * T, LANES)[:, :2], ri_s.reshape(DB, LANES)[:, :2]], axis=0).T
        row_token, tile_expert, n_used, dest = _dispatch(e_idx, n_tok)
        y_sorted = _expert_ffn(jnp.take(h2_all, row_token, axis=0), tile_expert, n_used,
                               moe_w_gate[l].astype(BF16), moe_w_up[l].astype(BF16), moe_w_down[l].astype(BF16))
        y0 = jnp.take(y_sorted, dest[0], axis=0)
        y1 = jnp.take(y_sorted, dest[1], axis=0)
        y_p = _combine(x1_p, y0[:B * T].reshape(B, T, D), y1[:B * T].reshape(B, T, D), rw_p, mod1_p,
                       ln2g, ln2b, alpha, tm)
        y_s = _combine(x1_s, y0[B * T:].reshape(1, DB, D), y1[B * T:].reshape(1, DB, D), rw_s, mod1_s,
                       ln2g, ln2b, alpha, DB)

    return (y_p, y_s.reshape(DB, 1, D),
            *[jnp.stack(a) for a in outs_p], *[jnp.stack(a) for a in outs_s])
```

```python
import functools

import jax
import jax.numpy as jnp
from jax import lax
from jax.experimental import pallas as pl
from jax.experimental.pallas import tpu as pltpu

F32 = jnp.float32
BF16 = jnp.bfloat16
I32 = jnp.int32
HIGHEST = lax.Precision.HIGHEST

HEAD_DIM = 64
N_HEADS = 4
ROT_DIM = HEAD_DIM // 4
ROPE_THETA = 500000.0
CMP_BLOCK = 32
SEL_BLOCK = 64
TOP_N = 8
WINDOW = 512
MLA_Q_LORA = 256
MLA_KV_LORA = 128
MLA_NOPE = 64
MLA_ROPE = 32
MLA_V = 64
BRANCH_W = 256
N_BRANCH = 3
N_GROUPS = 4
EXP_PER_GROUP = 8
N_EXPERTS = N_GROUPS * EXP_PER_GROUP
LN_EPS = 1e-5
NEG_INF = -1e30
MLA_SCALE = (MLA_NOPE + MLA_ROPE) ** -0.5
ATT_SCALE = HEAD_DIM ** -0.5

LANES = 128
SUBLANES = 8
VMEM_LIMIT = 56 * 1024 * 1024
ROW_TILE = 256
MOE_TILE = 256

SEG_Q = 0
SEG_KV = 256
SEG_G = 640
SEG_DQ = 768
SEG_LAT = 1024
SEG_ROPE = 1152
SEG_FQKV = 1280
SEG_FF = 2048
SEG_BG = 2176
MLA_QW = 256


def _cparams(sem):
    return pltpu.CompilerParams(dimension_semantics=sem, vmem_limit_bytes=VMEM_LIMIT)


def _iota(shape, dim, dtype=I32):
    return lax.broadcasted_iota(dtype, shape, dim)


def _nt_dot(a, b):
    return lax.dot_general(a, b, (((1,), (1,)), ((), ())), preferred_element_type=F32)


def _dot(a, b, precision=None):
    return jnp.dot(a, b, preferred_element_type=F32, precision=precision)


def _ada_kernel(c_ref, w_ref, b_ref, o_ref):
    c = c_ref[...]
    s = c * jax.nn.sigmoid(c)
    o_ref[...] = _dot(s, w_ref[...], HIGHEST) + b_ref[...]


def _ada(c_all, ada_w, ada_b):
    S, D, D3 = ada_w.shape
    Bc = c_all.shape[0]
    tn = D3 // 3
    return pl.pallas_call(
        _ada_kernel,
        grid=(S, D3 // tn),
        in_specs=[pl.BlockSpec((Bc, D), lambda s, n: (0, 0)),
                  pl.BlockSpec((None, D, tn), lambda s, n: (s, 0, n)),
                  pl.BlockSpec((None, 1, tn), lambda s, n: (s, 0, n))],
        out_specs=pl.BlockSpec((None, Bc, tn), lambda s, n: (s, 0, n)),
        out_shape=jax.ShapeDtypeStruct((S, Bc, D3), F32),
        compiler_params=_cparams(("arbitrary", "arbitrary")),
        name="ada",
    )(c_all, ada_w, ada_b.reshape(S, 1, D3))


def _rot_tables(pos):
    posf = pos.astype(F32)[:, None]
    lane = jnp.arange(LANES)

    def table(period, half, rot_dim, active):
        i = lane % period
        in_rot = (i < rot_dim) & active
        inv = 1.0 / (ROPE_THETA ** ((i % half).astype(F32) * 2.0 / rot_dim))
        ang = posf * inv[None, :]
        cos = jnp.where(in_rot[None, :], jnp.cos(ang), 1.0)
        sin = jnp.where(in_rot[None, :], jnp.where((i < half)[None, :], -jnp.sin(ang), jnp.sin(ang)), 0.0)
        return [cos, sin]

    h = ROT_DIM // 2
    t = (table(HEAD_DIM, h, ROT_DIM, lane >= 0)
         + table(HEAD_DIM, h, ROT_DIM, (lane % LANES) < HEAD_DIM)
         + table(LANES, MLA_ROPE // 2, MLA_ROPE, lane >= 0)
         + table(MLA_ROPE, MLA_ROPE // 2, MLA_ROPE, lane >= 0))
    return jnp.stack(t).astype(F32)


def _inproj_kernel(x_ref, sh_ref, sc_ref, rot_ref, w_ref, qn_ref, wuq_ref, wqc_ref, kvn_ref, fbf_ref,
                   qn_o, rows_o, win_o, selwin_o, gate_o, mq_o, lat_o, rope_o, mkey_o,
                   fq_o, frows_o, fkv_o, logf_o, bg_o, fvt_o, latt_o, swt_o):
    D = x_ref.shape[-1]
    h = x_ref[...] * (1.0 + sc_ref[...]) + sh_ref[...]
    hb = h.astype(BF16)
    lane = _iota((1, LANES), 1)

    def proj(a, width):
        return _dot(hb, w_ref[:, a:a + width])

    def rot(xc, t, half, period):
        cos = rot_ref[2 * t]
        sin = rot_ref[2 * t + 1]
        fwd = pltpu.roll(xc, LANES - half, 1)
        bwd = pltpu.roll(xc, half, 1)
        first = (lane & (period - 1)) < half
        return xc * cos + jnp.where(first, fwd, bwd) * sin

    hr = ROT_DIM // 2
    q = proj(SEG_Q, 256)
    for c in range(2):
        qc = rot(q[:, c * LANES:(c + 1) * LANES], 0, hr, HEAD_DIM)
        qn_o[:, c * LANES:(c + 1) * LANES] = (qc * ATT_SCALE).astype(BF16)
    kv = proj(SEG_KV, 384)
    for c in range(3):
        r = rot(kv[:, c * LANES:(c + 1) * LANES], 1, hr, HEAD_DIM)
        if c < 2:
            rows_o[:, c * LANES:(c + 1) * LANES] = r
        else:
            win_o[...] = r
        if c >= 1:
            selwin_o[:, (c - 1) * LANES:c * LANES] = r.astype(BF16)
            swt_o[(c - 1) * LANES:c * LANES, :] = r.T.astype(BF16)
    gate_o[...] = jax.nn.sigmoid(proj(SEG_G, LANES))
    dq = proj(SEG_DQ, MLA_Q_LORA)
    dqn = dq * lax.rsqrt(jnp.mean(dq * dq, axis=-1, keepdims=True) + LN_EPS) * qn_ref[...]
    qh = _dot(dqn.astype(BF16), wuq_ref[...])
    qrope = rot(qh[:, 256:384], 3, MLA_ROPE // 2, MLA_ROPE)
    mq = _dot(qh[:, :256].astype(BF16), wqc_ref[0:256, :]) + _dot(qrope.astype(BF16), wqc_ref[256:384, :])
    mq_o[...] = (mq * MLA_SCALE).astype(BF16)
    lat = proj(SEG_LAT, 256)
    latc = lat[:, :LANES]
    latn = latc * lax.rsqrt(jnp.mean(latc * latc, axis=-1, keepdims=True) + LN_EPS) * kvn_ref[...]
    ropec = rot(lat[:, LANES:], 2, MLA_ROPE // 2, MLA_ROPE)
    lat_o[...] = latn
    rope_o[...] = ropec[:, :MLA_ROPE]
    mkey_o[:, :LANES] = latn.astype(BF16)
    mkey_o[:, LANES:] = ropec.astype(BF16)
    latt_o[...] = latn.T.astype(BF16)
    fqkv = proj(SEG_FQKV, 768)
    fq_o[...] = (fqkv[:, :256] * ATT_SCALE).astype(BF16)
    frows_o[...] = fqkv[:, 256:]
    fkv_o[...] = fqkv[:, 256:].astype(BF16)
    fvt_o[...] = fqkv[:, 512:].T.astype(BF16)
    z = proj(SEG_FF, LANES) + fbf_ref[...]
    logf_o[...] = jnp.minimum(z, 0.0) - jnp.log(1.0 + jnp.exp(-jnp.abs(z)))
    for c in range(N_BRANCH):
        bg_o[:, c * D:(c + 1) * D] = jax.nn.sigmoid(proj(SEG_BG + c * D, D)).astype(BF16)


def _inproj(x, mod, rot, wts, tm):
    G, R, D = x.shape
    Rm = mod.shape[1]
    Rt = rot.shape[1]
    tmm = tm if Rm > 1 else 1
    tmt = tm if Rt > 1 else 1
    w_in, qn, wuq, wqc, kvn, fbf = wts
    widths = [(256, BF16), (256, F32), (128, F32), (256, BF16), (128, F32), (N_HEADS * MLA_QW, BF16),
              (128, F32), (MLA_ROPE, F32), (256, BF16), (256, BF16), (512, F32), (512, BF16), (128, F32),
              (N_BRANCH * D, BF16)]
    const = lambda g, i: (0, 0)
    row = lambda g, i: (g, i, 0)
    return pl.pallas_call(
        _inproj_kernel,
        grid=(G, R // tm),
        in_specs=[pl.BlockSpec((None, tm, D), row),
                  pl.BlockSpec((None, tmm, D), (lambda g, i: (g, i, 0)) if Rm > 1 else (lambda g, i: (g, 0, 0))),
                  pl.BlockSpec((None, tmm, D), (lambda g, i: (g, i, 1)) if Rm > 1 else (lambda g, i: (g, 0, 1))),
                  pl.BlockSpec((8, tmt, LANES), (lambda g, i: (0, i, 0)) if Rt > 1 else (lambda g, i: (0, 0, 0))),
                  pl.BlockSpec(w_in.shape, const),
                  pl.BlockSpec(qn.shape, const),
                  pl.BlockSpec(wuq.shape, const),
                  pl.BlockSpec(wqc.shape, const),
                  pl.BlockSpec(kvn.shape, const),
                  pl.BlockSpec(fbf.shape, const)],
        out_specs=([pl.BlockSpec((None, tm, w), row) for w, _ in widths]
                   + [pl.BlockSpec((None, w, tm), lambda g, i: (g, 0, i)) for w in (256, MLA_KV_LORA, 256)]),
        out_shape=([jax.ShapeDtypeStruct((G, R, w), dt) for w, dt in widths]
                   + [jax.ShapeDtypeStruct((G, w, R), BF16) for w in (256, MLA_KV_LORA, 256)]),
        compiler_params=_cparams(("arbitrary", "arbitrary")),
        name="inproj",
    )(x, mod, mod, rot, w_in, qn, wuq, wqc, kvn, fbf)


def _cumsum_kernel(lf_ref, fcol_ref, frow_ref, carry):
    tc = lf_ref.shape[0]

    @pl.when(pl.program_id(1) == 0)
    def _():
        carry[...] = jnp.zeros_like(carry)

    lower = (_iota((tc, tc), 1) <= _iota((tc, tc), 0)).astype(F32)
    f = _dot(lower, lf_ref[...], HIGHEST) + carry[...]
    fcol_ref[...] = f
    carry[...] = f[tc - 1:tc, :]
    frow_ref[...] = f.T[0:SUBLANES, :]


def _fox_cumsum(logf, tc):
    B, T, _ = logf.shape
    return pl.pallas_call(
        _cumsum_kernel,
        grid=(B, T // tc),
        in_specs=[pl.BlockSpec((None, tc, LANES), lambda b, i: (b, i, 0))],
        out_specs=[pl.BlockSpec((None, tc, LANES), lambda b, i: (b, i, 0)),
                   pl.BlockSpec((None, SUBLANES, tc), lambda b, i: (b, 0, i))],
        out_shape=[jax.ShapeDtypeStruct((B, T, LANES), F32),
                   jax.ShapeDtypeStruct((B, SUBLANES, T), F32)],
        scratch_shapes=[pltpu.VMEM((1, LANES), F32)],
        compiler_params=_cparams(("arbitrary", "arbitrary")),
        name="fox_cumsum",
    )(logf)


def _softmax_step(s, mask, m, l, axis):
    if mask is not None:
        s = jnp.where(mask, s, NEG_INF)
    m_new = jnp.maximum(m, jnp.max(s, axis=axis, keepdims=True))
    alpha = jnp.exp(m - m_new)
    p = jnp.exp(s - m_new)
    if mask is not None:
        p = jnp.where(mask, p, 0.0)
    l = alpha * l + jnp.sum(p, axis=axis, keepdims=True)
    return m_new, l, alpha, p.astype(BF16)


def _attend_heads(scores, mask, carry, axis, values):
    mid = [_softmax_step(s, mask, c[0], c[1], axis) for s, c in zip(scores, carry)]
    return tuple((m, l, alpha * c[2] + values(h, p)) for h, ((m, l, alpha, p), c) in enumerate(zip(mid, carry)))


def _normalize(acc, l):
    return jnp.where(l > 0.0, acc / jnp.where(l > 0.0, l, 1.0), 0.0)


def _fox_kernel(q_ref, k_ref, vt_ref, fcol_ref, frow_ref, o_ref, *, tk):
    tq = q_ref.shape[0]
    i = pl.program_id(1)
    qpos = i * tq + _iota((1, tq), 1)
    assert tq == tk
    qs = [q_ref[:, h * HEAD_DIM:(h + 1) * HEAD_DIM] for h in range(N_HEADS)]
    fqs = [frow_ref[h:h + 1, :] for h in range(N_HEADS)]

    def tile(j, carry, diagonal):
        ks = pl.multiple_of(j * tk, tk)
        mask = ((ks + _iota((tk, 1), 0)) <= qpos) if diagonal else None
        scores = []
        for h in range(N_HEADS):
            k = k_ref[pl.ds(ks, tk), h * HEAD_DIM:(h + 1) * HEAD_DIM]
            scores.append(_nt_dot(k, qs[h]) + (fqs[h] - fcol_ref[pl.ds(ks, tk), h:h + 1]))
        return _attend_heads(scores, mask, carry, 0,
                             lambda h, p: _dot(vt_ref[h * HEAD_DIM:(h + 1) * HEAD_DIM, pl.ds(ks, tk)], p))

    init = tuple((jnp.full((1, tq), NEG_INF, F32), jnp.zeros((1, tq), F32), jnp.zeros((HEAD_DIM, tq), F32))
                 for _ in range(N_HEADS))
    st = lax.fori_loop(0, i, lambda j, c: tile(j, c, False), init)
    st = tile(i, st, True)
    o_t = jnp.concatenate([_normalize(acc, l) for _, l, acc in st], axis=0)
    o_ref[...] = o_t.T.astype(o_ref.dtype)


def _fox_prompt(q, kv, v_t, fcol, frow, tq):
    B, T, _ = q.shape
    return pl.pallas_call(
        functools.partial(_fox_kernel, tk=tq),
        grid=(B, T // tq),
        in_specs=[pl.BlockSpec((None, tq, 256), lambda b, i: (b, i, 0)),
                  pl.BlockSpec((None, T, 256), lambda b, i: (b, 0, 0)),
                  pl.BlockSpec((None, 256, T), lambda b, i: (b, 0, 0)),
                  pl.BlockSpec((None, T, LANES), lambda b, i: (b, 0, 0)),
                  pl.BlockSpec((None, SUBLANES, tq), lambda b, i: (b, 0, i))],
        out_specs=pl.BlockSpec((None, tq, 256), lambda b, i: (b, i, 0)),
        out_shape=jax.ShapeDtypeStruct((B, T, 256), BF16),
        compiler_params=_cparams(("arbitrary", "arbitrary")),
        name="fox_prompt",
    )(q, kv, v_t, fcol, frow)


def _mla_kernel(q_ref, key_ref, latt_ref, o_ref, *, tk):
    tq = q_ref.shape[0]
    i = pl.program_id(1)
    qpos = i * tq + _iota((1, tq), 1)
    assert tq == tk
    qs = [q_ref[:, h * MLA_QW:(h + 1) * MLA_QW] for h in range(N_HEADS)]

    def tile(j, carry, diagonal):
        ks = pl.multiple_of(j * tk, tk)
        mask = ((ks + _iota((tk, 1), 0)) <= qpos) if diagonal else None
        k = key_ref[pl.ds(ks, tk), :]
        scores = [_nt_dot(k, qs[h]) for h in range(N_HEADS)]
        return _attend_heads(scores, mask, carry, 0, lambda h, p: _dot(latt_ref[:, pl.ds(ks, tk)], p))

    init = tuple((jnp.full((1, tq), NEG_INF, F32), jnp.zeros((1, tq), F32), jnp.zeros((MLA_KV_LORA, tq), F32))
                 for _ in range(N_HEADS))
    st = lax.fori_loop(0, i, lambda j, c: tile(j, c, False), init)
    st = tile(i, st, True)
    o_t = jnp.concatenate([_normalize(acc, l) for _, l, acc in st], axis=0)
    o_ref[...] = o_t.T.astype(o_ref.dtype)


def _mla_prompt(q, key, lat_t, tq):
    B, T, _ = q.shape
    return pl.pallas_call(
        functools.partial(_mla_kernel, tk=tq),
        grid=(B, T // tq),
        in_specs=[pl.BlockSpec((None, tq, N_HEADS * MLA_QW), lambda b, i: (b, i, 0)),
                  pl.BlockSpec((None, T, MLA_QW), lambda b, i: (b, 0, 0)),
                  pl.BlockSpec((None, MLA_KV_LORA, T), lambda b, i: (b, 0, 0))],
        out_specs=pl.BlockSpec((None, tq, N_HEADS * MLA_KV_LORA), lambda b, i: (b, i, 0)),
        out_shape=jax.ShapeDtypeStruct((B, T, N_HEADS * MLA_KV_LORA), BF16),
        compiler_params=_cparams(("arbitrary", "arbitrary")),
        name="mla_prompt",
    )(q, key, lat_t)


def _select_blocks(score, blk_f, n_sel):
    nb = score.shape[-1]
    sel = jnp.zeros(score.shape, jnp.bool_)
    picks = []
    for _ in range(n_sel):
        mx = jnp.max(score, axis=-1, keepdims=True)
        idx = jnp.min(jnp.where(score == mx, blk_f, float(nb)), axis=-1, keepdims=True)
        pick = blk_f == idx
        sel = jnp.logical_or(sel, pick)
        score = jnp.where(pick, -jnp.inf, score)
        picks.append(idx)
    return sel, picks


def _block_scores(imp, blk, qpos):
    cur = qpos >> 6
    valid = blk * SEL_BLOCK <= qpos
    forced = (blk == 0) | (blk == cur) | (blk == cur - 1)
    return jnp.where(forced & valid, 1e6, jnp.where(valid, imp, -1e6))


def _nsa_kernel(q_ref, rows_ref, selwin_ref, swt_ref, gate_ref, cmpt_ref, o_ref, ckv_ref, *, tk):
    tq = q_ref.shape[0]
    T = rows_ref.shape[0]
    nbc = T // CMP_BLOCK
    nbs = T // SEL_BLOCK
    i = pl.program_id(1)

    @pl.when(i == 0)
    def _():
        ct = cmpt_ref[...]
        e = jnp.exp(ct - jnp.max(ct, axis=-1, keepdims=True))
        a = e / (jnp.sum(e, axis=-1, keepdims=True) / float(nbc))
        inblk = (_iota((nbc, T), 1) >> 5) == _iota((nbc, T), 0)
        rows = rows_ref[...]
        ck = _dot(jnp.where(inblk, a[0:1, :], 0.0), rows, HIGHEST)
        cv = _dot(jnp.where(inblk, a[1:2, :], 0.0), rows, HIGHEST)
        ckv_ref[...] = jnp.where(_iota((1, LANES), 1) < HEAD_DIM, ck, cv)

    qpos = i * tq + _iota((tq, 1), 0)
    ckv = ckv_ref[...].astype(BF16)
    ck = ckv[:, :HEAD_DIM]
    cv = ckv[:, HEAD_DIM:]
    cblk = _iota((1, nbc), 1)
    cmask = (cblk + 1) * CMP_BLOCK - 1 <= qpos
    psum = jnp.zeros((tq, nbc), F32)
    qs = [q_ref[:, h * HEAD_DIM:(h + 1) * HEAD_DIM] for h in range(N_HEADS)]
    cmp_scores = [_nt_dot(q, ck) for q in qs]
    cmp_probs = []
    for h in range(N_HEADS):
        s = jnp.where(cmask, cmp_scores[h], NEG_INF)
        p = jnp.where(cmask, jnp.exp(s - jnp.max(s, axis=-1, keepdims=True)), 0.0)
        den = jnp.sum(p, axis=-1, keepdims=True)
        p = p * jnp.where(den > 0.0, 1.0 / jnp.where(den > 0.0, den, 1.0), 0.0)
        cmp_probs.append(p.astype(BF16))
        psum = psum + p
    o_c = [_dot(p, cv) for p in cmp_probs]
    pair = ((_iota((nbc, nbs), 0) >> 1) == _iota((nbc, nbs), 1)).astype(F32)
    imp = _dot(psum, pair, HIGHEST)
    blk = _iota((1, nbs), 1)
    sel, _ = _select_blocks(_block_scores(imp, blk, qpos), blk.astype(F32), min(TOP_N, nbs))
    sel_b = sel.astype(BF16)

    qpos_row = i * tq + _iota((1, tq), 1)

    def init():
        return tuple((jnp.full((1, tq), NEG_INF, F32), jnp.zeros((1, tq), F32),
                      jnp.zeros((HEAD_DIM, tq), F32)) for _ in range(N_HEADS))

    def sel_body(j, carry):
        ks = pl.multiple_of(j * tk, tk)
        kpos = ks + _iota((tk, 1), 0)
        expand = (((ks + _iota((tk, nbs), 0)) >> 6) == _iota((tk, nbs), 1)).astype(BF16)
        mask = (_nt_dot(expand, sel_b) > 0.5) & (kpos <= qpos_row)
        k = selwin_ref[pl.ds(ks, tk), 0:HEAD_DIM]
        return _attend_heads([_nt_dot(k, q) for q in qs], mask, carry, 0,
                             lambda h, p: _dot(swt_ref[HEAD_DIM:2 * HEAD_DIM, pl.ds(ks, tk)], p))

    n_kv = (i * tq) // tk + tq // tk
    st_s = lax.fori_loop(0, n_kv, sel_body, init())

    def win_body(j, carry):
        ks = pl.multiple_of(j * tk, tk)
        dist = qpos_row - (ks + _iota((tk, 1), 0))
        mask = (dist >= 0) & (dist <= WINDOW)
        k = selwin_ref[pl.ds(ks, tk), 2 * HEAD_DIM:3 * HEAD_DIM]
        return _attend_heads([_nt_dot(k, q) for q in qs], mask, carry, 0,
                             lambda h, p: _dot(swt_ref[3 * HEAD_DIM:4 * HEAD_DIM, pl.ds(ks, tk)], p))

    first = jnp.maximum((i * tq - WINDOW) // tk, 0)
    st_w = lax.fori_loop(first, n_kv, win_body, init())

    o_s = jnp.concatenate([_normalize(acc, l) for _, l, acc in st_s], axis=0).T
    o_w = jnp.concatenate([_normalize(acc, l) for _, l, acc in st_w], axis=0).T
    g = gate_ref[...]
    outs = []
    for h in range(N_HEADS):
        cols = slice(h * HEAD_DIM, (h + 1) * HEAD_DIM)
        outs.append(g[:, h:h + 1] * o_c[h] + g[:, N_HEADS + h:N_HEADS + h + 1] * o_s[:, cols]
                    + g[:, 2 * N_HEADS + h:2 * N_HEADS + h + 1] * o_w[:, cols])
    o_ref[...] = jnp.concatenate(outs, axis=-1).astype(o_ref.dtype)


def _nsa_prompt(q, rows, selwin, selwin_t, gates, cmp_t, tq):
    B, T, _ = q.shape
    assert T % SEL_BLOCK == 0 and T % tq == 0
    return pl.pallas_call(
        functools.partial(_nsa_kernel, tk=tq),
        grid=(B, T // tq),
        in_specs=[pl.BlockSpec((None, tq, 256), lambda b, i: (b, i, 0)),
                  pl.BlockSpec((None, T, LANES), lambda b, i: (b, 0, 0)),
                  pl.BlockSpec((None, T, 256), lambda b, i: (b, 0, 0)),
                  pl.BlockSpec((None, 256, T), lambda b, i: (b, 0, 0)),
                  pl.BlockSpec((None, tq, LANES), lambda b, i: (b, i, 0)),
                  pl.BlockSpec((2, T), lambda b, i: (0, 0))],
        out_specs=pl.BlockSpec((None, tq, 256), lambda b, i: (b, i, 0)),
        out_shape=jax.ShapeDtypeStruct((B, T, 256), BF16),
        scratch_shapes=[pltpu.VMEM((T // CMP_BLOCK, LANES), F32)],
        compiler_params=_cparams(("arbitrary", "arbitrary")),
        name="nsa_prompt",
    )(q, rows, selwin, selwin_t, gates, cmp_t)


FOX_PAGES_PER_STEP = 32
MLA_PAGES_PER_STEP = 64


def _pages_per_step(n_pages, cap):
    g = min(n_pages, cap)
    while n_pages % g:
        g -= 1
    return g


def _fox_dec_kernel(pt_ref, q_ref, new_ref, lfn_ref, *refs, G):
    kv_refs = refs[:G]
    lf_refs = refs[G:2 * G]
    o_ref = refs[2 * G]
    m_ref, l_ref, acc_ref, car_ref = refs[2 * G + 1:]
    s_id = pl.program_id(1)
    q = q_ref[...]

    @pl.when(s_id == 0)
    def _():
        new = new_ref[...]
        m_ref[...] = jnp.sum(q * new[:, :256], axis=-1, keepdims=True)
        l_ref[...] = jnp.ones_like(l_ref)
        acc_ref[...] = jnp.broadcast_to(new[:, 256:], acc_ref.shape)
        car_ref[...] = lfn_ref[...]

    qb = q.astype(BF16)
    later = (_iota((LANES, LANES), 0) > _iota((LANES, LANES), 1)).astype(F32)
    lf_all = jnp.concatenate([lf_refs[g][...] for g in range(G)], axis=0)
    within = _dot(lf_all, later, HIGHEST)
    total = jnp.sum(lf_all, axis=-1, keepdims=True)
    carry = car_ref[...]
    parts = []
    for g in range(G):
        rows = slice(g * SUBLANES, (g + 1) * SUBLANES)
        parts.append(_dot(qb, kv_refs[g][0].astype(BF16)) + within[rows] + carry)
        carry = carry + total[rows]
    car_ref[...] = carry
    s = jnp.concatenate(parts, axis=-1)
    m_old = m_ref[...]
    m_new = jnp.maximum(m_old, jnp.max(s, axis=-1, keepdims=True))
    alpha = jnp.exp(m_old - m_new)
    p = jnp.exp(s - m_new)
    l_ref[...] = alpha * l_ref[...] + jnp.sum(p, axis=-1, keepdims=True)
    m_ref[...] = m_new
    pb = p.astype(BF16)
    pv = None
    for g in range(G):
        t = _nt_dot(pb[:, g * LANES:(g + 1) * LANES], kv_refs[g][1].astype(BF16))
        pv = t if pv is None else pv + t
    acc_ref[...] = alpha * acc_ref[...] + pv

    @pl.when(s_id == pl.num_programs(1) - 1)
    def _():
        o = acc_ref[...] / l_ref[...]
        own = (_iota(o.shape, 1) >> 6) == _iota(o.shape, 0)
        o_ref[...] = jnp.sum(jnp.where(own, o, 0.0), axis=0, keepdims=True)


def _fox_decode(layer, page_table, q_bd, new_row, lf_new, cache_kv, cache_lf_t):
    DB, NP = page_table.shape
    G = _pages_per_step(NP, FOX_PAGES_PER_STEP)
    pt = page_table.reshape(-1)

    def page(g, ndim):
        return lambda b, s, pt: (layer, pt[b * NP + NP - 1 - (s * G + g)]) + (0,) * ndim

    grid_spec = pltpu.PrefetchScalarGridSpec(
        num_scalar_prefetch=1,
        grid=(DB, NP // G),
        in_specs=([pl.BlockSpec((None, SUBLANES, 256), lambda b, s, pt: (b, 0, 0)),
                   pl.BlockSpec((None, 1, 512), lambda b, s, pt: (b, 0, 0)),
                   pl.BlockSpec((None, SUBLANES, 1), lambda b, s, pt: (b, 0, 0))]
                  + [pl.BlockSpec((None, None, 2, 256, LANES), page(g, 3)) for g in range(G)]
                  + [pl.BlockSpec((None, None, SUBLANES, LANES), page(g, 2)) for g in range(G)]),
        out_specs=pl.BlockSpec((None, 1, 256), lambda b, s, pt: (b, 0, 0)),
        scratch_shapes=[pltpu.VMEM((SUBLANES, 1), F32), pltpu.VMEM((SUBLANES, 1), F32),
                        pltpu.VMEM((SUBLANES, 256), F32), pltpu.VMEM((SUBLANES, 1), F32)])
    return pl.pallas_call(
        functools.partial(_fox_dec_kernel, G=G),
        grid_spec=grid_spec,
        out_shape=jax.ShapeDtypeStruct((DB, 1, 256), F32),
        compiler_params=_cparams(("arbitrary", "arbitrary")),
        name="fox_decode",
    )(pt, q_bd, new_row, lf_new, *([cache_kv] * G), *([cache_lf_t] * G))


def _mla_dec_kernel(pt_ref, q_ref, new_ref, *refs, G):
    lat_refs = refs[:G]
    rope_refs = refs[G:2 * G]
    o_ref = refs[2 * G]
    m_ref, l_ref, acc_ref = refs[2 * G + 1:]
    s_id = pl.program_id(1)
    q = q_ref[...]

    @pl.when(s_id == 0)
    def _():
        new = new_ref[...]
        m_ref[...] = jnp.sum(q * new, axis=-1, keepdims=True)
        l_ref[...] = jnp.ones_like(l_ref)
        acc_ref[...] = jnp.broadcast_to(new[:, :MLA_KV_LORA], acc_ref.shape)

    q_lat = q[:, :MLA_KV_LORA].astype(BF16)
    q_rope = q[:, MLA_KV_LORA:].astype(BF16)
    rope_pad = jnp.zeros((LANES - MLA_ROPE, LANES), F32)
    parts = []
    for g in range(G):
        rope_t = jnp.concatenate([rope_refs[g][...], rope_pad], axis=0).astype(BF16)
        parts.append(_nt_dot(q_lat, lat_refs[g][...].astype(BF16)) + _dot(q_rope, rope_t))
    s = jnp.concatenate(parts, axis=-1)
    m_old = m_ref[...]
    m_new = jnp.maximum(m_old, jnp.max(s, axis=-1, keepdims=True))
    alpha = jnp.exp(m_old - m_new)
    p = jnp.exp(s - m_new)
    l_ref[...] = alpha * l_ref[...] + jnp.sum(p, axis=-1, keepdims=True)
    m_ref[...] = m_new
    pb = p.astype(BF16)
    pv = None
    for g in range(G):
        t = _dot(pb[:, g * LANES:(g + 1) * LANES], lat_refs[g][...].astype(BF16))
        pv = t if pv is None else pv + t
    acc_ref[...] = alpha * acc_ref[...] + pv

    @pl.when(s_id == pl.num_programs(1) - 1)
    def _():
        o_ref[...] = acc_ref[...] / l_ref[...]


def _mla_decode(layer, page_table, q_h, new_key, cache_lat, cache_rope):
    DB, NP = page_table.shape
    G = _pages_per_step(NP, MLA_PAGES_PER_STEP)
    pt = page_table.reshape(-1)

    def page(g):
        return lambda b, s, pt: (layer, pt[b * NP + s * G + g], 0, 0)

    grid_spec = pltpu.PrefetchScalarGridSpec(
        num_scalar_prefetch=1,
        grid=(DB, NP // G),
        in_specs=([pl.BlockSpec((None, SUBLANES, MLA_QW), lambda b, s, pt: (b, 0, 0)),
                   pl.BlockSpec((None, 1, MLA_QW), lambda b, s, pt: (b, 0, 0))]
                  + [pl.BlockSpec((None, None, LANES, MLA_KV_LORA), page(g)) for g in range(G)]
                  + [pl.BlockSpec((None, None, MLA_ROPE, LANES), page(g)) for g in range(G)]),
        out_specs=pl.BlockSpec((None, SUBLANES, MLA_KV_LORA), lambda b, s, pt: (b, 0, 0)),
        scratch_shapes=[pltpu.VMEM((SUBLANES, 1), F32), pltpu.VMEM((SUBLANES, 1), F32),
                        pltpu.VMEM((SUBLANES, MLA_KV_LORA), F32)])
    return pl.pallas_call(
        functools.partial(_mla_dec_kernel, G=G),
        grid_spec=grid_spec,
        out_shape=jax.ShapeDtypeStruct((DB, SUBLANES, MLA_KV_LORA), F32),
        compiler_params=_cparams(("arbitrary", "arbitrary")),
        name="mla_decode",
    )(pt, q_h, new_key, *([cache_lat] * G), *([cache_rope] * G))


def _nsa_cmp_kernel(pt_ref, q_ref, cmpt_ref, *refs, G, past_len):
    page_refs = refs[:G]
    sel_ref, oc_ref = refs[G:G + 2]
    ckv_ref = refs[G + 2]
    per_page = LANES // CMP_BLOCK
    ct = cmpt_ref[...]
    e = jnp.exp(ct - jnp.max(ct, axis=-1, keepdims=True))
    a = e / (jnp.sum(e, axis=-1, keepdims=True) / float(per_page))
    wrow = _iota((4 * per_page, LANES), 0)
    is_v_row = ((wrow >> 2) & 1) == 1
    w = jnp.where((_iota((4 * per_page, LANES), 1) >> 5) == (wrow & 3), jnp.where(is_v_row, a[1:2, :], a[0:1, :]), 0.0)
    w_hi = w.astype(BF16).astype(F32)
    wb = jnp.where(wrow < 2 * per_page, w_hi, w - w_hi).astype(BF16)
    is_k = _iota((1, LANES), 1) < HEAD_DIM
    for g in range(G):
        page = page_refs[g][...].reshape(2 * HEAD_DIM, LANES).astype(BF16)
        r = _nt_dot(wb, page)
        t = r[0:2 * per_page] + r[2 * per_page:]
        ckv_ref[g * per_page:(g + 1) * per_page, :] = jnp.where(is_k, t[0:per_page], t[per_page:])

    nbc = ckv_ref.shape[0]
    nb_past = nbc // (SEL_BLOCK // CMP_BLOCK)
    width = sel_ref.shape[-1]
    qpos = past_len
    ckv = ckv_ref[...].astype(BF16)
    q = q_ref[...].astype(BF16)
    cmask = (_iota((1, nbc), 1) + 1) * CMP_BLOCK - 1 <= qpos
    s = jnp.where(cmask, _nt_dot(q, ckv[:, :HEAD_DIM]), NEG_INF)
    p = jnp.where(cmask, jnp.exp(s - jnp.max(s, axis=-1, keepdims=True)), 0.0)
    den = jnp.sum(p, axis=-1, keepdims=True)
    p = p * jnp.where(den > 0.0, 1.0 / jnp.where(den > 0.0, den, 1.0), 0.0)
    oc_ref[...] = _dot(p.astype(BF16), ckv[:, HEAD_DIM:])
    head = _iota((SUBLANES, 1), 0) < N_HEADS
    psum = jnp.sum(jnp.where(head, p, 0.0), axis=0, keepdims=True)
    pair = ((_iota((nbc, width), 0) >> 1) == _iota((nbc, width), 1)).astype(F32)
    imp = _dot(jnp.broadcast_to(psum, (SUBLANES, nbc)), pair, HIGHEST)[0:1, :]
    blk = _iota((1, width), 1)
    n_sel = min(TOP_N, nb_past + 1)
    _, picks = _select_blocks(_block_scores(imp, blk, qpos), blk.astype(F32), n_sel)
    out = jnp.full((1, width), -1.0, F32)
    for r, idx in enumerate(picks):
        out = jnp.where(blk == r, idx, out)
    sel_ref[...] = out.astype(I32)


def _nsa_cmp_decode(layer, page_table, q_h, cmp_page, cache_t, past_len):
    DB, NP = page_table.shape
    pt = page_table.reshape(-1)
    nbc = NP * (LANES // CMP_BLOCK)
    nb_past = nbc // 2
    width = (nb_past + 1 + LANES - 1) // LANES * LANES

    def page(g):
        return lambda b, pt: (layer, pt[b * NP + g], 0, 0, 0)

    grid_spec = pltpu.PrefetchScalarGridSpec(
        num_scalar_prefetch=1,
        grid=(DB,),
        in_specs=([pl.BlockSpec((None, SUBLANES, HEAD_DIM), lambda b, pt: (b, 0, 0)),
                   pl.BlockSpec((2, LANES), lambda b, pt: (0, 0))]
                  + [pl.BlockSpec((None, None, 2, HEAD_DIM, LANES), page(g)) for g in range(NP)]),
        out_specs=[pl.BlockSpec((None, 1, width), lambda b, pt: (b, 0, 0)),
                   pl.BlockSpec((None, SUBLANES, HEAD_DIM), lambda b, pt: (b, 0, 0))],
        scratch_shapes=[pltpu.VMEM((nbc, LANES), F32)])
    return pl.pallas_call(
        functools.partial(_nsa_cmp_kernel, G=NP, past_len=past_len),
        grid_spec=grid_spec,
        out_shape=[jax.ShapeDtypeStruct((DB, 1, width), I32),
                   jax.ShapeDtypeStruct((DB, SUBLANES, HEAD_DIM), F32)],
        compiler_params=_cparams(("arbitrary",)),
        name="nsa_cmp_decode",
    )(pt, q_h, cmp_page, *([cache_t] * NP))


def _nsa_sel_kernel(pt_ref, sel_ref, q_ref, oc_ref, new_ref, gate_ref, *refs, n_sel, nb_past):
    blk_refs = refs[:n_sel]
    win_ref, o_ref = refs[n_sel:]
    b = pl.program_id(0)
    q = q_ref[...]
    qb = q.astype(BF16)
    new = new_ref[...]
    half_of_lane = _iota((1, LANES), 1) >> 6
    parts, masks = [], []
    n_new = jnp.int32(0)
    for r in range(n_sel):
        blk = sel_ref[b * n_sel + r]
        half = jnp.where((blk >= 0) & (blk < nb_past), blk & 1, -1)
        masks.append(half_of_lane == jnp.full((1, LANES), half, I32))
        parts.append(_dot(qb, blk_refs[r][0].astype(BF16)))
        n_new = n_new + jnp.where(blk == nb_past, 1, 0)
    has_new = jnp.full((1, 1), n_new, I32) > 0

    def attend(s, mask, k_new, v_new, new_ok, values):
        if mask is not None:
            s = jnp.where(mask, s, NEG_INF)
        s_new = jnp.where(new_ok, jnp.sum(q * k_new, axis=-1, keepdims=True), NEG_INF)
        m = jnp.maximum(jnp.max(s, axis=-1, keepdims=True), s_new)
        p = jnp.exp(s - m)
        if mask is not None:
            p = jnp.where(mask, p, 0.0)
        p_new = jnp.where(new_ok, jnp.exp(s_new - m), 0.0)
        l = jnp.sum(p, axis=-1, keepdims=True) + p_new
        return _normalize(values(p.astype(BF16)) + p_new * v_new, l)

    def sel_values(pb):
        pv = None
        for r in range(n_sel):
            t = _nt_dot(pb[:, r * LANES:(r + 1) * LANES], blk_refs[r][1].astype(BF16))
            pv = t if pv is None else pv + t
        return pv

    o_s = attend(jnp.concatenate(parts, axis=-1), jnp.concatenate(masks, axis=-1),
                 new[:, 0:HEAD_DIM], new[:, HEAD_DIM:2 * HEAD_DIM], has_new, sel_values)
    o_w = attend(_dot(qb, win_ref[0].astype(BF16)), None,
                 new[:, 2 * HEAD_DIM:3 * HEAD_DIM], new[:, 3 * HEAD_DIM:], jnp.ones((1, 1), jnp.bool_),
                 lambda pb: _nt_dot(pb, win_ref[1].astype(BF16)))
    g = gate_ref[...]
    o_ref[...] = g[:, 0:1] * oc_ref[...] + g[:, 1:2] * o_s + g[:, 2:3] * o_w


def _nsa_sel_decode(layer, page_table, sel, q_h, o_c, new_row, gates_t, cache_t, win_t):
    DB, NP = page_table.shape
    nb_past = NP * (LANES // SEL_BLOCK)
    n_sel = min(TOP_N, nb_past + 1)
    WB = win_t.shape[-1]
    pt = page_table.reshape(-1)
    sel_flat = sel[:, 0, :n_sel].reshape(-1)

    def block(r):
        def index(b, pt, sel):
            blk = jnp.clip(sel[b * n_sel + r], 0, nb_past - 1)
            return (layer, pt[b * NP + blk // 2], 1, 0, 0)
        return index

    row = lambda b, pt, sel: (b, 0, 0)
    grid_spec = pltpu.PrefetchScalarGridSpec(
        num_scalar_prefetch=2,
        grid=(DB,),
        in_specs=([pl.BlockSpec((None, SUBLANES, HEAD_DIM), row),
                   pl.BlockSpec((None, SUBLANES, HEAD_DIM), row),
                   pl.BlockSpec((None, 1, 256), row),
                   pl.BlockSpec((None, SUBLANES, LANES), row)]
                  + [pl.BlockSpec((None, None, 2, HEAD_DIM, LANES), block(r)) for r in range(n_sel)]
                  + [pl.BlockSpec((None, None, 2, HEAD_DIM, WB), lambda b, pt, sel: (layer, b, 0, 0, 0))]),
        out_specs=pl.BlockSpec((None, SUBLANES, HEAD_DIM), row))
    return pl.pallas_call(
        functools.partial(_nsa_sel_kernel, n_sel=n_sel, nb_past=nb_past),
        grid_spec=grid_spec,
        out_shape=jax.ShapeDtypeStruct((DB, SUBLANES, HEAD_DIM), F32),
        compiler_params=_cparams(("arbitrary",)),
        name="nsa_sel_decode",
    )(pt, sel_flat, q_h, o_c, new_row, gates_t, *([cache_t] * n_sel), win_t)


def _layernorm(y, g, b):
    mu = jnp.mean(y, axis=-1, keepdims=True)
    d = y - mu
    var = jnp.mean(d * d, axis=-1, keepdims=True)
    return d * lax.rsqrt(var + LN_EPS) * g + b


def _merge_kernel(x_ref, onsa_ref, olat_ref, ofox_ref, bg_ref, g1_ref, sh2_ref, sc2_ref,
                  wuv_ref, wb_ref, wo_ref, lng_ref, lnb_ref, wr_ref, br_ref,
                  x1_o, h2_o, ri_o, rw_o, *, alpha):
    D = x_ref.shape[-1]
    o_mla = _dot(olat_ref[...], wuv_ref[...]).astype(BF16)
    branches = (onsa_ref[...], o_mla, ofox_ref[...])
    mix = None
    for n in range(N_BRANCH):
        t = bg_ref[:, n * D:(n + 1) * D].astype(F32) * _dot(branches[n], wb_ref[n])
        mix = t if mix is None else mix + t
    mix = _dot(mix.astype(BF16), wo_ref[...])
    x1 = _layernorm(alpha * x_ref[...] + g1_ref[...] * mix, lng_ref[...], lnb_ref[...])
    x1_o[...] = x1
    h2 = x1 * (1.0 + sc2_ref[...]) + sh2_ref[...]
    h2_o[...] = h2.astype(BF16)
    logits = _dot(h2, wr_ref[...], HIGHEST) + br_ref[...]
    lane = _iota((1, LANES), 1)
    lane_f = lane.astype(F32)
    is_g = lane < N_GROUPS
    gmax = jnp.max(jnp.where(is_g, logits, NEG_INF), axis=-1, keepdims=True)
    gidx = jnp.min(jnp.where(is_g & (logits == gmax), lane_f, float(LANES)), axis=-1, keepdims=True)
    g_w = 1.0 / jnp.sum(jnp.where(is_g, jnp.exp(logits - gmax), 0.0), axis=-1, keepdims=True)
    grp_of_lane = ((lane - N_GROUPS) >> 3).astype(F32)
    in_grp = (lane >= N_GROUPS) & (lane < N_GROUPS + N_EXPERTS) & (grp_of_lane == gidx)
    v1 = jnp.max(jnp.where(in_grp, logits, NEG_INF), axis=-1, keepdims=True)
    i1 = jnp.min(jnp.where(in_grp & (logits == v1), lane_f, float(LANES)), axis=-1, keepdims=True)
    rest = in_grp & (lane_f != i1)
    v2 = jnp.max(jnp.where(rest, logits, NEG_INF), axis=-1, keepdims=True)
    i2 = jnp.min(jnp.where(rest & (logits == v2), lane_f, float(LANES)), axis=-1, keepdims=True)
    e21 = jnp.exp(v2 - v1)
    w1 = g_w / (1.0 + e21)
    w2 = g_w * e21 / (1.0 + e21)
    ri_o[...] = jnp.where(lane == 0, i1 - N_GROUPS, jnp.where(lane == 1, i2 - N_GROUPS, 0.0)).astype(I32)
    rw_o[...] = jnp.where(lane == 0, w1, jnp.where(lane == 1, w2, 0.0))


def _merge(x, o_nsa, o_lat, o_fox, bg, mod0, mod1, wts, alpha, tm):
    G, R, D = x.shape
    Rm = mod0.shape[1]
    tmm = tm if Rm > 1 else 1
    wuv, wb, wo, lng, lnb, wr, br = wts
    row = lambda g, i: (g, i, 0)

    def modspec(j):
        return pl.BlockSpec((None, tmm, D), (lambda g, i: (g, i, j)) if Rm > 1 else (lambda g, i: (g, 0, j)))

    def const(a):
        return pl.BlockSpec(a.shape, lambda g, i: (0,) * a.ndim)

    return pl.pallas_call(
        functools.partial(_merge_kernel, alpha=alpha),
        grid=(G, R // tm),
        in_specs=[pl.BlockSpec((None, tm, D), row),
                  pl.BlockSpec((None, tm, 256), row),
                  pl.BlockSpec((None, tm, N_HEADS * MLA_KV_LORA), row),
                  pl.BlockSpec((None, tm, 256), row),
                  pl.BlockSpec((None, tm, N_BRANCH * D), row),
                  modspec(2), ] + [pl.BlockSpec((None, tmm, D), (lambda g, i: (g, i, 0)) if Rm > 1 else (lambda g, i: (g, 0, 0))),
                                   pl.BlockSpec((None, tmm, D), (lambda g, i: (g, i, 1)) if Rm > 1 else (lambda g, i: (g, 0, 1)))]
                 + [const(a) for a in (wuv, wb, wo, lng, lnb, wr, br)],
        out_specs=[pl.BlockSpec((None, tm, D), row), pl.BlockSpec((None, tm, D), row),
                   pl.BlockSpec((None, tm, LANES), row), pl.BlockSpec((None, tm, LANES), row)],
        out_shape=[jax.ShapeDtypeStruct((G, R, D), F32), jax.ShapeDtypeStruct((G, R, D), BF16),
                   jax.ShapeDtypeStruct((G, R, LANES), I32), jax.ShapeDtypeStruct((G, R, LANES), F32)],
        compiler_params=_cparams(("arbitrary", "arbitrary")),
        name="merge",
    )(x, o_nsa, o_lat, o_fox, bg, mod0, mod1, mod1, wuv, wb, wo, lng, lnb, wr, br)


def _ffn_kernel(te_ref, nt_ref, x_ref, wg_ref, wu_ref, wd_ref, o_ref):
    i = pl.program_id(0)

    @pl.when(i < nt_ref[0])
    def _():
        x = x_ref[...]
        gte = _dot(x, wg_ref[...].astype(BF16))
        a = gte * jax.nn.sigmoid(gte) * _dot(x, wu_ref[...].astype(BF16))
        o_ref[...] = _dot(a.astype(BF16), wd_ref[...].astype(BF16)).astype(o_ref.dtype)

    @pl.when(i >= nt_ref[0])
    def _():
        o_ref[...] = jnp.zeros_like(o_ref)


def _expert_ffn(layer, x_sorted, tile_expert, n_tiles_used, wg, wu, wd):
    NP, D = x_sorted.shape
    FF = wg.shape[-1]
    grid_spec = pltpu.PrefetchScalarGridSpec(
        num_scalar_prefetch=2,
        grid=(NP // MOE_TILE,),
        in_specs=[pl.BlockSpec((MOE_TILE, D), lambda i, te, nt: (i, 0)),
                  pl.BlockSpec((None, None, D, FF), lambda i, te, nt: (layer, te[i], 0, 0)),
                  pl.BlockSpec((None, None, D, FF), lambda i, te, nt: (layer, te[i], 0, 0)),
                  pl.BlockSpec((None, None, FF, D), lambda i, te, nt: (layer, te[i], 0, 0))],
        out_specs=pl.BlockSpec((MOE_TILE, D), lambda i, te, nt: (i, 0)))
    return pl.pallas_call(
        _ffn_kernel,
        grid_spec=grid_spec,
        out_shape=jax.ShapeDtypeStruct((NP, D), BF16),
        compiler_params=_cparams(("arbitrary",)),
        name="expert_ffn",
    )(tile_expert, n_tiles_used, x_sorted, wg, wu, wd)


def _dispatch(e_idx, n_tok):
    n_asg = 2 * n_tok
    e_all = e_idx.reshape(-1)
    order = jnp.argsort(e_all, stable=True).astype(I32)
    counts = jnp.sum(e_all[:, None] == jnp.arange(N_EXPERTS, dtype=I32)[None, :], axis=0, dtype=I32)
    starts = jnp.cumsum(counts) - counts
    pcounts = (counts + MOE_TILE - 1) // MOE_TILE * MOE_TILE
    pends = jnp.cumsum(pcounts)
    pstarts = pends - pcounts
    n_tiles = (n_asg + MOE_TILE - 1) // MOE_TILE + N_EXPERTS
    tile_start = jnp.arange(n_tiles, dtype=I32) * MOE_TILE
    tile_expert = jnp.minimum(jnp.sum(tile_start[:, None] >= pends[None, :], axis=1, dtype=I32), N_EXPERTS - 1)
    rows = jnp.arange(n_tiles * MOE_TILE, dtype=I32)
    re = tile_expert[rows // MOE_TILE]
    local = rows - pstarts[re]
    valid = local < counts[re]
    src = jnp.where(valid, order[jnp.clip(starts[re] + local, 0, n_asg - 1)], 0)
    row_token = src % n_tok
    rank = jnp.zeros((n_asg,), I32).at[order].set(jnp.arange(n_asg, dtype=I32))
    dest = pstarts[e_all] + rank - starts[e_all]
    n_used = (pends[-1] // MOE_TILE).astype(I32).reshape(1)
    return row_token, tile_expert, n_used, dest.reshape(2, n_tok)


def _combine_kernel(x_ref, y0_ref, y1_ref, rw_ref, g2_ref, lng_ref, lnb_ref, o_ref, *, alpha):
    rw = rw_ref[...]
    y = rw[:, 0:1] * y0_ref[...].astype(F32) + rw[:, 1:2] * y1_ref[...].astype(F32)
    o_ref[...] = _layernorm(alpha * x_ref[...] + g2_ref[...] * y, lng_ref[...], lnb_ref[...])


def _combine(x1, y0, y1, rw, mod1, lng, lnb, alpha, tm):
    G, R, D = x1.shape
    Rm = mod1.shape[1]
    tmm = tm if Rm > 1 else 1
    row = lambda g, i: (g, i, 0)
    const = lambda g, i: (0, 0)
    return pl.pallas_call(
        functools.partial(_combine_kernel, alpha=alpha),
        grid=(G, R // tm),
        in_specs=[pl.BlockSpec((None, tm, D), row), pl.BlockSpec((None, tm, D), row),
                  pl.BlockSpec((None, tm, D), row), pl.BlockSpec((None, tm, LANES), row),
                  pl.BlockSpec((None, tmm, D), (lambda g, i: (g, i, 2)) if Rm > 1 else (lambda g, i: (g, 0, 2))),
                  pl.BlockSpec((1, D), const), pl.BlockSpec((1, D), const)],
        out_specs=pl.BlockSpec((None, tm, D), row),
        out_shape=jax.ShapeDtypeStruct((G, R, D), F32),
        compiler_params=_cparams(("arbitrary", "arbitrary")),
        name="combine",
    )(x1, y0, y1, rw, mod1, lng, lnb)


def _pad_cols(w, width):
    return jnp.pad(w, ((0, 0), (0, width - w.shape[1])))


def _prep_layer(l, w_in, mla_q_norm, mla_w_uq, mla_kv_norm, mla_w_uk, mla_w_uv, fox_bf, w_branch, w_o,
                ln1_g, ln1_b, moe_w_grp, moe_b_grp, moe_w_exp, moe_b_exp, ln2_g, ln2_b):
    w = w_in[l]
    D = w.shape[0]
    o = [0, 256, 640, 652, 908, 1036, 1068, 1836, 1840, 1840 + N_BRANCH * D]
    w_p = jnp.concatenate([w[:, o[0]:o[2]], _pad_cols(w[:, o[2]:o[3]], LANES), w[:, o[3]:o[5]],
                           _pad_cols(w[:, o[5]:o[6]], LANES), w[:, o[6]:o[7]],
                           _pad_cols(w[:, o[7]:o[8]], LANES), w[:, o[8]:o[9]]], axis=1).astype(BF16)
    uq = mla_w_uq[l].reshape(MLA_Q_LORA, N_HEADS, MLA_NOPE + MLA_ROPE)
    wuq = jnp.concatenate([uq[:, :, :MLA_NOPE].reshape(MLA_Q_LORA, -1),
                           uq[:, :, MLA_NOPE:].reshape(MLA_Q_LORA, -1)], axis=1).astype(BF16)
    uk = mla_w_uk[l]
    wqc = jnp.zeros((N_HEADS * MLA_NOPE + N_HEADS * MLA_ROPE, N_HEADS * MLA_QW), F32)
    eye = jnp.eye(MLA_ROPE, dtype=F32)
    for h in range(N_HEADS):
        wqc = wqc.at[h * MLA_NOPE:(h + 1) * MLA_NOPE, h * MLA_QW:h * MLA_QW + MLA_KV_LORA].set(uk[:, h, :].T)
        r0 = N_HEADS * MLA_NOPE + h * MLA_ROPE
        c0 = h * MLA_QW + MLA_KV_LORA
        wqc = wqc.at[r0:r0 + MLA_ROPE, c0:c0 + MLA_ROPE].set(eye)
    uv = mla_w_uv[l]
    wuv = jnp.zeros((N_HEADS * MLA_KV_LORA, N_HEADS * MLA_V), F32)
    for h in range(N_HEADS):
        wuv = wuv.at[h * MLA_KV_LORA:(h + 1) * MLA_KV_LORA, h * MLA_V:(h + 1) * MLA_V].set(uv[:, h, :])
    fbf = _pad_cols(fox_bf[l][None, :], LANES)
    wr = _pad_cols(jnp.concatenate([moe_w_grp[l], moe_w_exp[l]], axis=1), LANES)
    br = _pad_cols(jnp.concatenate([moe_b_grp[l], moe_b_exp[l]])[None, :], LANES)
    inproj_w = (w_p, mla_q_norm[l][None, :], wuq, wqc.astype(BF16), mla_kv_norm[l][None, :], fbf)
    merge_w = (wuv.astype(BF16), w_branch[l].astype(BF16), w_o[l].astype(BF16),
               ln1_g[l][None, :], ln1_b[l][None, :], wr, br)
    return inproj_w, merge_w, (ln2_g[l][None, :], ln2_b[l][None, :])


def kernel(x_prompt, x_sample, cache_nsa_kv, state_nsa_win, cache_mla_latent, cache_mla_rope, cache_fox_kv, cache_fox_logf, page_table, c_prompt, c_sample, ada_w, ada_b, w_in, nsa_cmp_pos, mla_q_norm, mla_w_uq, mla_kv_norm, mla_w_uk, mla_w_uv, fox_bf, w_branch, w_o, ln1_g, ln1_b, moe_w_grp, moe_b_grp, moe_w_exp, moe_b_exp, moe_w_gate, moe_w_up, moe_w_down, ln2_g, ln2_b):
    B, T, D = x_prompt.shape
    DB = x_sample.shape[0]
    assert x_sample.shape[1] == 1
    depth = ada_w.shape[0]
    n_pool = cache_nsa_kv.shape[1]
    NP = page_table.shape[1]
    past_len = NP * LANES
    assert cache_nsa_kv.shape[2] == LANES
    alpha = (2 * depth) ** 0.25
    tm = min(ROW_TILE, T)
    assert T % tm == 0
    n_keep = min(WINDOW, T)

    c_all = jnp.concatenate([c_prompt, c_sample], axis=0)
    ada = _ada(c_all, ada_w.reshape(depth * 2, D, 3 * D), ada_b.reshape(depth * 2, 3 * D))

    rot_p = _rot_tables(jnp.arange(T, dtype=I32))
    rot_s = _rot_tables(jnp.full((1,), past_len, I32))
    nsa_t = jnp.transpose(cache_nsa_kv, (0, 1, 3, 4, 2))
    win_t = jnp.transpose(state_nsa_win, (0, 1, 3, 4, 2))
    fox_t = jnp.transpose(cache_fox_kv, (0, 1, 3, 4, 5, 2)).reshape(depth, n_pool, 2, 256, LANES)
    rope_t = jnp.transpose(cache_mla_rope, (0, 1, 3, 2))
    lf_t = jnp.pad(jnp.swapaxes(cache_fox_logf, 2, 3), ((0, 0), (0, 0), (0, SUBLANES - N_HEADS), (0, 0)))

    y_p = x_prompt
    y_s = x_sample.reshape(1, DB, D)
    outs_p = [[] for _ in range(6)]
    outs_s = [[] for _ in range(6)]
    head_pad = ((0, 0), (0, SUBLANES - N_HEADS), (0, 0))
    for l in range(depth):
        inproj_w, merge_w, (ln2g, ln2b) = _prep_layer(
            l, w_in, mla_q_norm, mla_w_uq, mla_kv_norm, mla_w_uk, mla_w_uv, fox_bf, w_branch, w_o,
            ln1_g, ln1_b, moe_w_grp, moe_b_grp, moe_w_exp, moe_b_exp, ln2_g, ln2_b)
        mod0_p = ada[2 * l, :B].reshape(B, 1, 3 * D)
        mod1_p = ada[2 * l + 1, :B].reshape(B, 1, 3 * D)
        mod0_s = ada[2 * l, B:].reshape(1, DB, 3 * D)
        mod1_s = ada[2 * l + 1, B:].reshape(1, DB, 3 * D)
        cmp_t = jnp.tile(nsa_cmp_pos[l], (1, T // CMP_BLOCK))
        cmp_page = jnp.tile(nsa_cmp_pos[l], (1, LANES // CMP_BLOCK))

        (qn, rows, win, selwin, gates, mq, lat, rope, mkey, fq, frows, fkv, logf, bg,
         fv_t, lat_t, selwin_t) = _inproj(y_p, mod0_p, rot_p, inproj_w, tm)
        fcol, frow = _fox_cumsum(logf, tm)
        o_nsa = _nsa_prompt(qn, rows, selwin, selwin_t, gates, cmp_t, tm)
        o_lat = _mla_prompt(mq, mkey, lat_t, tm)
        o_fox = _fox_prompt(fq, fkv, fv_t, fcol, frow, tm)
        x1_p, h2_p, ri_p, rw_p = _merge(y_p, o_nsa, o_lat, o_fox, bg, mod0_p, mod1_p, merge_w, alpha, tm)
        outs_p[0].append(rows.reshape(B, T, 4, HEAD_DIM))
        outs_p[1].append(win[:, T - n_keep:].reshape(B, n_keep, 2, HEAD_DIM))
        outs_p[2].append(lat)
        outs_p[3].append(rope)
        outs_p[4].append(frows.reshape(B, T, 2, N_HEADS, HEAD_DIM))
        outs_p[5].append(logf[:, :, :N_HEADS])

        (qn, rows, win, selwin, gates, mq, lat, rope, mkey, fq, frows, fkv, logf, bg, _, _, _) = _inproj(
            y_s, mod0_s, rot_s, inproj_w, DB)
        qn_h = jnp.pad(qn[0].astype(F32).reshape(DB, N_HEADS, HEAD_DIM), head_pad)
        sel, o_c = _nsa_cmp_decode(l, page_table, qn_h, cmp_page, nsa_t, past_len)
        new_sw = jnp.concatenate([rows[0][:, LANES:], win[0]], axis=-1).reshape(DB, 1, 256)
        gates_t = jnp.pad(jnp.swapaxes(gates[0][:, :N_BRANCH * N_HEADS].reshape(DB, N_BRANCH, N_HEADS), 1, 2),
                          ((0, 0), (0, SUBLANES - N_HEADS), (0, LANES - N_BRANCH)))
        o_nsa_s = _nsa_sel_decode(l, page_table, sel, qn_h, o_c, new_sw, gates_t, nsa_t, win_t)
        mq_h = jnp.pad(mq[0].astype(F32).reshape(DB, N_HEADS, MLA_QW), head_pad)
        o_lat_s = _mla_decode(l, page_table, mq_h, mkey[0].astype(F32).reshape(DB, 1, MLA_QW),
                              cache_mla_latent, rope_t)
        fq_h = fq[0].astype(F32).reshape(DB, N_HEADS, 1, HEAD_DIM)
        fq_bd = (fq_h * jnp.eye(N_HEADS, dtype=F32)[None, :, :, None]).reshape(DB, N_HEADS, 256)
        lf_new = jnp.pad(logf[0][:, :N_HEADS, None], head_pad)
        o_fox_s = _fox_decode(l, page_table, jnp.pad(fq_bd, head_pad), frows[0].reshape(DB, 1, 512), lf_new,
                              fox_t, lf_t)
        x1_s, h2_s, ri_s, rw_s = _merge(
            y_s, o_nsa_s[:, :N_HEADS].reshape(1, DB, 256).astype(BF16),
            o_lat_s[:, :N_HEADS].reshape(1, DB, N_HEADS * MLA_KV_LORA).astype(BF16),
            o_fox_s.reshape(1, DB, 256).astype(BF16), bg, mod0_s, mod1_s, merge_w, alpha, DB)
        outs_s[0].append(rows[0].reshape(DB, 1, 4, HEAD_DIM))
        outs_s[1].append(jnp.concatenate([state_nsa_win[l][:, 1:], win[0].reshape(DB, 1, 2, HEAD_DIM)], axis=1))
        outs_s[2].append(lat[0].reshape(DB, 1, MLA_KV_LORA))
        outs_s[3].append(rope[0].reshape(DB, 1, MLA_ROPE))
        outs_s[4].append(frows[0].reshape(DB, 1, 2, N_HEADS, HEAD_DIM))
        outs_s[5].append(logf[0][:, :N_HEADS].reshape(DB, 1, N_HEADS))

        n_tok = B * T + DB
        h2_all = jnp.concatenate([h2_p.reshape(B * T, D), h2_s.reshape(DB, D)], axis=0)
        e_idx = jnp.concatenate([ri_p.reshape(B * T, LANES)[:, :2], ri_s.reshape(DB, LANES)[:, :2]], axis=0).T
        row_token, tile_expert, n_used, dest = _dispatch(e_idx, n_tok)
        y_sorted = _expert_ffn(l, jnp.take(h2_all, row_token, axis=0), tile_expert, n_used,
                               moe_w_gate, moe_w_up, moe_w_down)
        y0 = jnp.take(y_sorted, dest[0], axis=0)
        y1 = jnp.take(y_sorted, dest[1], axis=0)
        y_p = _combine(x1_p, y0[:B * T].reshape(B, T, D), y1[:B * T].reshape(B, T, D), rw_p, mod1_p,
                       ln2g, ln2b, alpha, tm)
        y_s = _combine(x1_s, y0[B * T:].reshape(1, DB, D), y1[B * T:].reshape(1, DB, D), rw_s, mod1_s,
                       ln2g, ln2b, alpha, DB)

    return (y_p, y_s.reshape(DB, 1, D),
            *[jnp.stack(a) for a in outs_p], *[jnp.stack(a) for a in outs_s])
```
